```python
import jax, jax.numpy as jnp
from jax import lax
import numpy as np

D_MODEL = 1024
BATCH = 2
SEQ = 8192
DEPTH = 2

N_MIXERS = 2
HEAD_DIM = 64
N_Q_HEADS = D_MODEL // HEAD_DIM
N_KV_HEADS = N_Q_HEADS // 4
GROUP = N_Q_HEADS // N_KV_HEADS
ROT_DIM = HEAD_DIM // 4
ROPE_THETA = 500000.0
WINDOW = 128
BLOCK = 128
IDX_HEADS = 8
IDX_DIM = 64
IDX_ROT_DIM = IDX_DIM // 4
TOPK_MAX = 256
D_FF = 4 * D_MODEL
EPS = 1e-6
N_A = (DEPTH + 1) // 2
N_B = DEPTH // 2
QD = N_Q_HEADS * HEAD_DIM
KVD = N_KV_HEADS * HEAD_DIM
A_IN = QD + 2 * KVD
B_SPLITS = [QD, QD + KVD, QD + 2 * KVD, QD + 2 * KVD + IDX_HEADS * IDX_DIM,
            QD + 2 * KVD + IDX_HEADS * IDX_DIM + IDX_DIM]
B_IN = B_SPLITS[-1] + IDX_HEADS

kernel_name = "hybrid_swa_sink_dsa_trunk"


def rmsnorm(x, g):
    xf = x.astype(jnp.float32)
    y = xf * lax.rsqrt(jnp.mean(xf * xf, axis=-1, keepdims=True) + EPS)
    return (y * g.astype(jnp.float32)).astype(x.dtype)


def partial_rope(x, pos, rot_dim):
    inv = ROPE_THETA ** (-jnp.arange(0, rot_dim, 2, dtype=jnp.float32) / rot_dim)
    ang = pos.astype(jnp.float32)[..., None] * inv
    cos = jnp.cos(ang)[:, :, None, :]
    sin = jnp.sin(ang)[:, :, None, :]
    half = rot_dim // 2
    x1 = x[..., :half].astype(jnp.float32)
    x2 = x[..., half:rot_dim].astype(jnp.float32)
    rot = jnp.concatenate([x1 * cos - x2 * sin, x2 * cos + x1 * sin], axis=-1)
    return jnp.concatenate([rot.astype(x.dtype), x[..., rot_dim:]], axis=-1)


def sliding_window_sink_attention(h, pos, w_in, w_out, sinks):
    B, S, _ = h.shape
    nb = S // BLOCK
    proj = h @ w_in
    q, k, v = jnp.split(proj, [QD, QD + KVD], axis=-1)
    q = partial_rope(q.reshape(B, S, N_Q_HEADS, HEAD_DIM), pos, ROT_DIM)
    k = partial_rope(k.reshape(B, S, N_KV_HEADS, HEAD_DIM), pos, ROT_DIM)
    qb = q.reshape(B, nb, BLOCK, N_KV_HEADS, GROUP, HEAD_DIM)
    kb = k.reshape(B, nb, BLOCK, N_KV_HEADS, HEAD_DIM)
    vb = v.reshape(B, nb, BLOCK, N_KV_HEADS, HEAD_DIM)

    def with_prev(t):
        prev = jnp.concatenate([jnp.zeros_like(t[:, :1]), t[:, :-1]], axis=1)
        return jnp.concatenate([prev, t], axis=2)

    kk, vv = with_prev(kb), with_prev(vb)
    scores = jnp.einsum('bnqhgd,bnkhd->bnhgqk', qb, kk).astype(jnp.float32) * (HEAD_DIM ** -0.5)
    r = jnp.arange(BLOCK)[:, None] + BLOCK
    c = jnp.arange(2 * BLOCK)[None, :]
    diff = r - c
    within = (diff >= 0) & (diff < WINDOW)
    blk = jnp.arange(nb)[:, None, None]
    valid = within[None] & ((blk > 0) | (c[None] >= BLOCK))
    scores = jnp.where(valid[None, :, None, None], scores, -jnp.inf)
    sink = sinks.astype(jnp.float32).reshape(1, 1, N_KV_HEADS, GROUP, 1, 1)
    m = jnp.maximum(jnp.max(scores, axis=-1, keepdims=True), sink)
    p = jnp.exp(scores - m)
    probs = p / (jnp.sum(p, axis=-1, keepdims=True) + jnp.exp(sink - m))
    out = jnp.einsum('bnhgqk,bnkhd->bnqhgd', probs.astype(vv.dtype), vv)
    return out.reshape(B, S, QD) @ w_out


def dsa_sparse_attention(h, pos, w_in, w_out):
    B, S, _ = h.shape
    nb = S // BLOCK
    n_sel = min(TOPK_MAX, S // 4)
    proj = h @ w_in
    q, k, v, qi, ki, wi = jnp.split(proj, B_SPLITS, axis=-1)
    q = partial_rope(q.reshape(B, S, N_Q_HEADS, HEAD_DIM), pos, ROT_DIM)
    k = partial_rope(k.reshape(B, S, N_KV_HEADS, HEAD_DIM), pos, ROT_DIM)
    v = v.reshape(B, S, N_KV_HEADS, HEAD_DIM)
    qi = partial_rope(qi.reshape(B, S, IDX_HEADS, IDX_DIM), pos, IDX_ROT_DIM)
    ki = partial_rope(ki.reshape(B, S, 1, IDX_DIM), pos, IDX_ROT_DIM)[:, :, 0]
    wi = wi * ((IDX_HEADS ** -0.5) * (IDX_DIM ** -0.5))

    qb = jnp.moveaxis(q.reshape(B, nb, BLOCK, N_KV_HEADS, GROUP, HEAD_DIM), 1, 0)
    qib = jnp.moveaxis(qi.reshape(B, nb, BLOCK, IDX_HEADS, IDX_DIM), 1, 0)
    wib = jnp.moveaxis(wi.reshape(B, nb, BLOCK, IDX_HEADS), 1, 0)
    key_pos = jnp.arange(S)

    def block(args):
        q_blk, qi_blk, wi_blk, i = args
        t = i * BLOCK + jnp.arange(BLOCK)
        logits = jnp.einsum('bthd,bsd->bths', qi_blk, ki).astype(jnp.float32)
        score = jnp.einsum('bths,bth->bts', jax.nn.relu(logits), wi_blk.astype(jnp.float32))
        score = jnp.where(key_pos[None, None, :] <= t[None, :, None], score, -jnp.inf)
        _, idx = lax.top_k(score, n_sel)
        sel_ok = idx <= t[None, :, None]
        k_sel = jax.vmap(lambda kb_, ib_: kb_[ib_])(k, idx)
        v_sel = jax.vmap(lambda vb_, ib_: vb_[ib_])(v, idx)
        s = jnp.einsum('bthgd,btkhd->bhgtk', q_blk, k_sel).astype(jnp.float32) * (HEAD_DIM ** -0.5)
        s = jnp.where(sel_ok[:, None, None], s, -jnp.inf)
        pr = jax.nn.softmax(s, axis=-1)
        return jnp.einsum('bhgtk,btkhd->bthgd', pr.astype(v_sel.dtype), v_sel)

    outs = lax.map(block, (qb, qib, wib, jnp.arange(nb)))
    out = jnp.moveaxis(outs, 0, 1).reshape(B, S, QD)
    return out @ w_out


def squared_relu_mlp(h, w_up, w_down):
    a = jax.nn.relu(h @ w_up)
    return (a * a) @ w_down


def setup_inputs(seed: int = 0) -> dict:
    key = jax.random.key(seed)
    ks = jax.random.split(key, 10)
    f32 = jnp.float32
    x = jax.random.normal(ks[0], (BATCH, SEQ, D_MODEL), f32)
    positions = jnp.broadcast_to(jnp.arange(SEQ, dtype=jnp.int32)[None, :], (BATCH, SEQ))
    norm_gains = 1.0 + 0.05 * jax.random.normal(ks[1], (DEPTH, 4, D_MODEL), f32)
    w_mlp_up = jax.random.normal(ks[2], (DEPTH, D_MODEL, D_FF), f32) * D_MODEL ** -0.5
    w_mlp_down = jax.random.normal(ks[3], (DEPTH, D_FF, D_MODEL), f32) * D_FF ** -0.5
    a_w_in = jax.random.normal(ks[4], (N_A, D_MODEL, A_IN), f32) * D_MODEL ** -0.5
    a_w_out = jax.random.normal(ks[5], (N_A, QD, D_MODEL), f32) * QD ** -0.5
    a_sinks = 0.5 * jax.random.normal(ks[6], (N_A, N_Q_HEADS), f32)
    b_w_in = jax.random.normal(ks[7], (N_B, D_MODEL, B_IN), f32) * D_MODEL ** -0.5
    b_w_out = jax.random.normal(ks[8], (N_B, QD, D_MODEL), f32) * QD ** -0.5
    return {"x": x, "positions": positions, "norm_gains": norm_gains,
            "w_mlp_up": w_mlp_up, "w_mlp_down": w_mlp_down,
            "a_w_in": a_w_in, "a_w_out": a_w_out, "a_sinks": a_sinks,
            "b_w_in": b_w_in, "b_w_out": b_w_out}


def reference(x, positions, norm_gains, w_mlp_up, w_mlp_down, a_w_in, a_w_out, a_sinks, b_w_in, b_w_out):
    ia = 0
    ib = 0
    for layer in range(DEPTH):
        g = norm_gains[layer]
        hn = rmsnorm(x, g[0])
        if layer % N_MIXERS == 0:
            mix = sliding_window_sink_attention(hn, positions, a_w_in[ia], a_w_out[ia], a_sinks[ia])
            ia += 1
        else:
            mix = dsa_sparse_attention(hn, positions, b_w_in[ib], b_w_out[ib])
            ib += 1
        x = x + rmsnorm(mix, g[1])
        hn = rmsnorm(x, g[2])
        x = x + rmsnorm(squared_relu_mlp(hn, w_mlp_up[layer], w_mlp_down[layer]), g[3])
    return x
```

```python
import functools

import jax
import jax.numpy as jnp
import numpy as np
from jax import lax
from jax.experimental import pallas as pl
from jax.experimental.pallas import tpu as pltpu

D_MODEL = 1024
HEAD_DIM = 64
N_Q_HEADS = 16
N_KV_HEADS = 4
GROUP = 4
ROT_DIM = 16
ROT_HALF = ROT_DIM // 2
ROPE_THETA = 500000.0
BLOCK = 128
IDX_HEADS = 8
IDX_DIM = 64
TOPK_MAX = 256
D_FF = 4 * D_MODEL
EPS = 1e-6
QD = N_Q_HEADS * HEAD_DIM
KVD = N_KV_HEADS * HEAD_DIM

LANES = 128
SUBLANES = 8
TOK_TILE = 512
ROW_TILE = 512
FF_CHUNK = 1024
VMEM_LIMIT = 56 * 1024 * 1024
INT_MIN = -2 ** 31
MXU_DTYPE = jnp.bfloat16
NEG_INIT = -1e30

_NT = (((1,), (1,)), ((), ()))


def _rms(x, g):
    return x * lax.rsqrt(jnp.mean(x * x, axis=-1, keepdims=True) + EPS) * g


def _rope_table_kernel(pos_ref, inv_ref, cos_ref, sin_ref):
    ang = pos_ref[0].astype(jnp.float32) * inv_ref[:, :1]
    cos_ref[0] = jnp.cos(ang)
    sin_ref[0] = jnp.sin(ang)


def _rope_tables(positions):
    B, S = positions.shape
    inv = ROPE_THETA ** (-jnp.arange(0, ROT_DIM, 2, dtype=jnp.float32) / ROT_DIM)
    inv = jnp.broadcast_to(inv[:, None], (ROT_HALF, LANES))
    out = jax.ShapeDtypeStruct((B, ROT_HALF, S), jnp.float32)
    return pl.pallas_call(
        _rope_table_kernel,
        grid=(B,),
        in_specs=[pl.BlockSpec((1, 1, S), lambda b: (b, 0, 0)),
                  pl.BlockSpec((ROT_HALF, LANES), lambda b: (0, 0))],
        out_specs=[pl.BlockSpec((1, ROT_HALF, S), lambda b: (b, 0, 0))] * 2,
        out_shape=[out, out],
        name="rope_tables",
    )(positions.reshape(B, 1, S), inv)


def _rope_heads_t(y, n_heads, cos, sin, scale=None):
    parts = []
    for h in range(n_heads):
        r0 = h * HEAD_DIM
        x1 = y[r0:r0 + ROT_HALF]
        x2 = y[r0 + ROT_HALF:r0 + ROT_DIM]
        parts += [x1 * cos - x2 * sin, x2 * cos + x1 * sin, y[r0 + ROT_DIM:r0 + HEAD_DIM]]
    out = jnp.concatenate(parts, axis=0)
    return out if scale is None else out * scale


def _proj_kernel(x_ref, g_ref, wt_ref, cos_ref, sin_ref, *out_refs, with_indexer):
    hn = _rms(x_ref[0], g_ref[...]).astype(MXU_DTYPE)
    cos = cos_ref[0]
    sin = sin_ref[0]

    def proj(r0, rows):
        return lax.dot_general(wt_ref[r0:r0 + rows, :], hn, _NT,
                               preferred_element_type=jnp.float32)

    qt_ref, k_ref, vt_ref = out_refs[:3]
    q = proj(0, QD)
    qt_ref[0] = _rope_heads_t(q, N_Q_HEADS, cos, sin, HEAD_DIM ** -0.5).astype(MXU_DTYPE)
    k = _rope_heads_t(proj(QD, KVD), N_KV_HEADS, cos, sin)
    k_ref[0] = k.T.astype(MXU_DTYPE)
    vt_ref[0, 0] = proj(QD + KVD, KVD).astype(MXU_DTYPE)
    if with_indexer:
        qit_ref, ki_ref, wit_ref = out_refs[3:]
        r0 = QD + 2 * KVD
        qi = _rope_heads_t(proj(r0, IDX_HEADS * IDX_DIM), IDX_HEADS, cos, sin)
        qit_ref[0] = qi.astype(MXU_DTYPE)
        r0 += IDX_HEADS * IDX_DIM
        ki = proj(r0, 2 * IDX_DIM)
        ki = jnp.concatenate([_rope_heads_t(ki[:IDX_DIM], 1, cos, sin), ki[IDX_DIM:]], axis=0)
        ki_ref[0] = ki.T.astype(MXU_DTYPE)
        r0 += 2 * IDX_DIM
        wi = proj(r0, 2 * SUBLANES)
        wit_ref[0] = wi[:IDX_HEADS] * ((IDX_HEADS ** -0.5) * (IDX_DIM ** -0.5))


def _project(x, g, wt, cos_t, sin_t, with_indexer):
    B, S, D = x.shape
    tm = TOK_TILE
    n_rows = wt.shape[0]
    bf = MXU_DTYPE
    out_shape = [jax.ShapeDtypeStruct((B, QD, S), bf),
                 jax.ShapeDtypeStruct((B, S, KVD), bf),
                 jax.ShapeDtypeStruct((B, S // tm, KVD, tm), bf)]
    out_specs = [pl.BlockSpec((1, QD, tm), lambda b, i: (b, 0, i)),
                 pl.BlockSpec((1, tm, KVD), lambda b, i: (b, i, 0)),
                 pl.BlockSpec((1, 1, KVD, tm), lambda b, i: (b, i, 0, 0))]
    if with_indexer:
        out_shape += [jax.ShapeDtypeStruct((B, IDX_HEADS * IDX_DIM, S), bf),
                      jax.ShapeDtypeStruct((B, S, 2 * IDX_DIM), bf),
                      jax.ShapeDtypeStruct((B, IDX_HEADS, S), jnp.float32)]
        out_specs += [pl.BlockSpec((1, IDX_HEADS * IDX_DIM, tm), lambda b, i: (b, 0, i)),
                      pl.BlockSpec((1, tm, 2 * IDX_DIM), lambda b, i: (b, i, 0)),
                      pl.BlockSpec((1, IDX_HEADS, tm), lambda b, i: (b, 0, i))]
    return pl.pallas_call(
        functools.partial(_proj_kernel, with_indexer=with_indexer),
        grid=(B, S // tm),
        in_specs=[pl.BlockSpec((1, tm, D), lambda b, i: (b, i, 0)),
                  pl.BlockSpec((1, D), lambda b, i: (0, 0)),
                  pl.BlockSpec((n_rows, D), lambda b, i: (0, 0)),
                  pl.BlockSpec((1, ROT_HALF, tm), lambda b, i: (b, 0, i)),
                  pl.BlockSpec((1, ROT_HALF, tm), lambda b, i: (b, 0, i))],
        out_specs=out_specs,
        out_shape=out_shape,
        compiler_params=pltpu.CompilerParams(
            dimension_semantics=("arbitrary", "arbitrary"), vmem_limit_bytes=VMEM_LIMIT),
        name="proj_indexer" if with_indexer else "proj",
    )(x, g.reshape(1, D), wt, cos_t, sin_t)


def _head_weights(qt, h):
    blocks = [qt[(GROUP * h + g) * HEAD_DIM:(GROUP * h + g + 1) * HEAD_DIM, :] for g in range(GROUP)]
    w = jnp.concatenate(blocks, axis=1)
    z = jnp.zeros_like(w)
    return jnp.concatenate([w, z] if h % 2 == 0 else [z, w], axis=0)


def _col_reduce(x, op):
    rows, n = x.shape
    part = op(x.reshape(rows // SUBLANES, SUBLANES, n), axis=0)
    return op(part, axis=0, keepdims=True)


def _swa_kernel(qt_ref, kp_ref, kc_ref, vp_ref, vc_ref, sink_ref, o_ref):
    n = pl.program_id(1)
    qt = qt_ref[0]
    c = lax.broadcasted_iota(jnp.int32, (2 * BLOCK, BLOCK), 0)
    i = lax.broadcasted_iota(jnp.int32, (2 * BLOCK, BLOCK), 1)
    first_slot = jnp.where(n > 0, 0, BLOCK)
    valid = (c > i) & (c <= i + BLOCK) & (c >= first_slot)
    bias = jnp.where(valid, 0.0, -jnp.inf).astype(jnp.float32)
    bias4 = jnp.concatenate([bias] * GROUP, axis=1)
    outs = []
    for h in range(N_KV_HEADS):
        pair = slice(LANES * (h // 2), LANES * (h // 2 + 1))
        kcat = jnp.concatenate([kp_ref[0, :, pair], kc_ref[0, :, pair]], axis=0)
        s = jnp.dot(kcat, _head_weights(qt, h), preferred_element_type=jnp.float32) + bias4
        sink = sink_ref[:, GROUP * h * BLOCK:GROUP * (h + 1) * BLOCK]
        m = jnp.maximum(_col_reduce(s, jnp.max), sink)
        p = jnp.exp(s - m)
        denom = _col_reduce(p, jnp.sum) + jnp.exp(sink - m)
        rows = slice(HEAD_DIM * h, HEAD_DIM * (h + 1))
        vcat = jnp.concatenate([vp_ref[0, 0, rows, :], vc_ref[0, 0, rows, :]], axis=1)
        o = jnp.dot(vcat, p.astype(MXU_DTYPE), preferred_element_type=jnp.float32) / denom
        outs += [o[:, g * BLOCK:(g + 1) * BLOCK] for g in range(GROUP)]
    o_ref[0] = jnp.concatenate(outs, axis=0).T.astype(MXU_DTYPE)


def _swa_attention(qt, k, vt, sinks):
    B, _, S = qt.shape
    nb = S // BLOCK
    per_tile = TOK_TILE // BLOCK
    sink_row = jnp.repeat(sinks.astype(jnp.float32), BLOCK).reshape(1, N_Q_HEADS * BLOCK)

    def prev(n):
        return jnp.maximum(n - 1, 0)

    return pl.pallas_call(
        _swa_kernel,
        grid=(B, nb),
        in_specs=[pl.BlockSpec((1, QD, BLOCK), lambda b, n: (b, 0, n)),
                  pl.BlockSpec((1, BLOCK, KVD), lambda b, n: (b, prev(n), 0)),
                  pl.BlockSpec((1, BLOCK, KVD), lambda b, n: (b, n, 0)),
                  pl.BlockSpec((1, 1, KVD, BLOCK), lambda b, n: (b, prev(n) // per_tile, 0, prev(n) % per_tile)),
                  pl.BlockSpec((1, 1, KVD, BLOCK), lambda b, n: (b, n // per_tile, 0, n % per_tile)),
                  pl.BlockSpec((1, N_Q_HEADS * BLOCK), lambda b, n: (0, 0))],
        out_specs=pl.BlockSpec((1, BLOCK, QD), lambda b, n: (b, n, 0)),
        out_shape=jax.ShapeDtypeStruct((B, S, QD), MXU_DTYPE),
        compiler_params=pltpu.CompilerParams(
            dimension_semantics=("arbitrary", "arbitrary"), vmem_limit_bytes=VMEM_LIMIT),
        name="swa_attention",
    )(qt, k, k, vt, vt, sink_row)


def _dsa_kernel(qt_ref, qit_ref, wit_ref, k_ref, ki_ref, vt_ref, o_ref,
                skey_ref, wq_ref, m_ref, l_ref, acc_ref, tie_ref, *, n_sel):
    n = pl.program_id(1)
    kc = TOK_TILE
    n_chunks = n // (kc // BLOCK) + 1
    idx_bits = (skey_ref.shape[0] - 1).bit_length()
    t_row = n * BLOCK + lax.broadcasted_iota(jnp.int32, (1, BLOCK), 1)

    def key_index(j):
        return j * kc + lax.broadcasted_iota(jnp.int32, (kc, BLOCK), 0)

    qt = qt_ref[0]
    for h in range(N_KV_HEADS):
        wq_ref[h] = _head_weights(qt, h)
    qit = qit_ref[0]
    wqi = jnp.concatenate([qit[h * IDX_DIM:(h + 1) * IDX_DIM, :] for h in range(IDX_HEADS)], axis=1)
    wqi = jnp.concatenate([wqi, jnp.zeros_like(wqi)], axis=0)
    wit = wit_ref[0]

    def score_chunk(j, carry):
        rows = pl.ds(pl.multiple_of(j * kc, kc), kc)
        lg = jnp.dot(ki_ref[0, rows, :], wqi, preferred_element_type=jnp.float32)
        score = jnp.zeros((kc, BLOCK), jnp.float32)
        for h in range(IDX_HEADS):
            score = score + jnp.maximum(lg[:, h * BLOCK:(h + 1) * BLOCK], 0.0) * wit[h:h + 1, :]
        bits = pltpu.bitcast(score, jnp.int32)
        skey = bits ^ ((bits >> 31) & 0x7FFFFFFF)
        skey = jnp.where(bits == INT_MIN, 0, skey)
        skey_ref[rows, :] = jnp.where(key_index(j) <= t_row, skey, INT_MIN)
        return carry

    lax.fori_loop(0, n_chunks, score_chunk, 0)

    def count(pred):
        def body(j, acc):
            rows = pl.ds(pl.multiple_of(j * kc, kc), kc)
            hit = jnp.where(pred(skey_ref[rows, :], j), 1, 0)
            return acc + jnp.sum(hit.reshape(kc // SUBLANES, SUBLANES, BLOCK), axis=0)
        acc = lax.fori_loop(0, n_chunks, body, jnp.zeros((SUBLANES, BLOCK), jnp.int32))
        return jnp.sum(acc, axis=0, keepdims=True)

    def bit_step(b, carry):
        thr_u, cnt_ge = carry
        cand_u = thr_u | lax.shift_left(jnp.int32(1), 31 - b)
        cand_s = cand_u ^ INT_MIN
        cnt = count(lambda sk, j: sk >= cand_s)
        ok = cnt >= n_sel
        return jnp.where(ok, cand_u, thr_u), jnp.where(ok, cnt, cnt_ge)

    zero_row = jnp.zeros((1, BLOCK), jnp.int32)
    thr_u, cnt_ge = lax.fori_loop(0, 32, bit_step, (zero_row, zero_row))
    thr = jnp.maximum(thr_u ^ INT_MIN, INT_MIN + 1)
    excess = cnt_ge > n_sel
    tie_ref[...] = jnp.full((1, BLOCK), 2 ** 30, jnp.int32)

    @pl.when(jnp.max(jnp.where(excess, 1, 0)) > 0)
    def _():
        need = n_sel - count(lambda sk, j: sk > thr)

        def idx_step(b, x):
            cand = x | lax.shift_left(jnp.int32(1), idx_bits - 1 - b)
            below = count(lambda sk, j: (sk == thr) & (key_index(j) < cand))
            return jnp.where(below < need, cand, x)

        x = lax.fori_loop(0, idx_bits, idx_step, zero_row)
        tie_ref[...] = jnp.where(excess, x, 2 ** 30)

    tie_x = tie_ref[...]

    m_ref[...] = jnp.full(m_ref.shape, NEG_INIT, jnp.float32)
    l_ref[...] = jnp.zeros(l_ref.shape, jnp.float32)
    acc_ref[...] = jnp.zeros(acc_ref.shape, jnp.float32)

    def attend_chunk(j, carry):
        rows = pl.ds(pl.multiple_of(j * kc, kc), kc)
        sk = skey_ref[rows, :]
        sel = (sk > thr) | ((sk == thr) & (key_index(j) <= tie_x))
        bias = jnp.where(sel, 0.0, -jnp.inf).astype(jnp.float32)
        bias4 = jnp.concatenate([bias] * GROUP, axis=1)
        for h in range(N_KV_HEADS):
            pair = slice(LANES * (h // 2), LANES * (h // 2 + 1))
            s = jnp.dot(k_ref[0, rows, pair], wq_ref[h], preferred_element_type=jnp.float32) + bias4
            m_old = m_ref[h]
            m_new = jnp.maximum(m_old, _col_reduce(s, jnp.max))
            alpha = jnp.exp(m_old - m_new)
            p = jnp.exp(s - m_new)
            l_ref[h] = alpha * l_ref[h] + _col_reduce(p, jnp.sum)
            vt = vt_ref[0, j, HEAD_DIM * h:HEAD_DIM * (h + 1), :]
            acc_ref[h] = alpha * acc_ref[h] + jnp.dot(vt, p.astype(MXU_DTYPE),
                                                      preferred_element_type=jnp.float32)
            m_ref[h] = m_new
        return carry

    lax.fori_loop(0, n_chunks, attend_chunk, 0)

    outs = []
    for h in range(N_KV_HEADS):
        o = acc_ref[h] / l_ref[h]
        outs += [o[:, g * BLOCK:(g + 1) * BLOCK] for g in range(GROUP)]
    o_ref[0] = jnp.concatenate(outs, axis=0).T.astype(MXU_DTYPE)


def _dsa_attention(qt, qit, wit, k, ki, vt):
    B, _, S = qt.shape
    nb = S // BLOCK
    n_sel = min(TOPK_MAX, S // 4)
    return pl.pallas_call(
        functools.partial(_dsa_kernel, n_sel=n_sel),
        grid=(B, nb),
        in_specs=[pl.BlockSpec((1, QD, BLOCK), lambda b, n: (b, 0, n)),
                  pl.BlockSpec((1, IDX_HEADS * IDX_DIM, BLOCK), lambda b, n: (b, 0, n)),
                  pl.BlockSpec((1, IDX_HEADS, BLOCK), lambda b, n: (b, 0, n)),
                  pl.BlockSpec((1, S, KVD), lambda b, n: (b, 0, 0)),
                  pl.BlockSpec((1, S, 2 * IDX_DIM), lambda b, n: (b, 0, 0)),
                  pl.BlockSpec((1, S // TOK_TILE, KVD, TOK_TILE), lambda b, n: (b, 0, 0, 0))],
        out_specs=pl.BlockSpec((1, BLOCK, QD), lambda b, n: (b, n, 0)),
        out_shape=jax.ShapeDtypeStruct((B, S, QD), MXU_DTYPE),
        scratch_shapes=[pltpu.VMEM((S, BLOCK), jnp.int32),
                        pltpu.VMEM((N_KV_HEADS, LANES, GROUP * BLOCK), MXU_DTYPE),
                        pltpu.VMEM((N_KV_HEADS, 1, GROUP * BLOCK), jnp.float32),
                        pltpu.VMEM((N_KV_HEADS, 1, GROUP * BLOCK), jnp.float32),
                        pltpu.VMEM((N_KV_HEADS, HEAD_DIM, GROUP * BLOCK), jnp.float32),
                        pltpu.VMEM((1, BLOCK), jnp.int32)],
        compiler_params=pltpu.CompilerParams(
            dimension_semantics=("arbitrary", "arbitrary"), vmem_limit_bytes=VMEM_LIMIT),
        name="dsa_attention",
    )(qt, qit, wit, k, ki, vt)


def _post_kernel(a_ref, x_ref, g_ref, wo_ref, wu_ref, wd_ref, o_ref):
    mix = jnp.dot(a_ref[...], wo_ref[...], preferred_element_type=jnp.float32)
    x1 = x_ref[...] + _rms(mix, g_ref[1:2, :])
    hn = _rms(x1, g_ref[2:3, :]).astype(MXU_DTYPE)
    acc = jnp.zeros(x1.shape, jnp.float32)
    for c in range(D_FF // FF_CHUNK):
        cols = slice(c * FF_CHUNK, (c + 1) * FF_CHUNK)
        a = jnp.maximum(jnp.dot(hn, wu_ref[:, cols], preferred_element_type=jnp.float32), 0.0)
        acc = acc + jnp.dot((a * a).astype(MXU_DTYPE), wd_ref[cols, :],
                            preferred_element_type=jnp.float32)
    o_ref[...] = x1 + _rms(acc, g_ref[3:4, :])


def _post_attention(attn, x, g, w_out, w_up, w_down):
    T, D = x.shape
    tm = ROW_TILE
    return pl.pallas_call(
        _post_kernel,
        grid=(T // tm,),
        in_specs=[pl.BlockSpec((tm, QD), lambda i: (i, 0)),
                  pl.BlockSpec((tm, D), lambda i: (i, 0)),
                  pl.BlockSpec((4, D), lambda i: (0, 0)),
                  pl.BlockSpec((QD, D), lambda i: (0, 0), pipeline_mode=pl.Buffered(1)),
                  pl.BlockSpec((D, D_FF), lambda i: (0, 0), pipeline_mode=pl.Buffered(1)),
                  pl.BlockSpec((D_FF, D), lambda i: (0, 0), pipeline_mode=pl.Buffered(1))],
        out_specs=pl.BlockSpec((tm, D), lambda i: (i, 0)),
        out_shape=jax.ShapeDtypeStruct((T, D), jnp.float32),
        compiler_params=pltpu.CompilerParams(
            dimension_semantics=("arbitrary",), vmem_limit_bytes=VMEM_LIMIT),
        name="post_attention",
    )(attn, x, g, w_out, w_up, w_down)


def _pad_rows(w, rows):
    return jnp.concatenate([w, jnp.zeros((rows - w.shape[0], w.shape[1]), w.dtype)], axis=0)


def kernel(x, positions, norm_gains, w_mlp_up, w_mlp_down, a_w_in, a_w_out, a_sinks, b_w_in, b_w_out):
    B, S, D = x.shape
    bf = MXU_DTYPE
    cos_t, sin_t = _rope_tables(positions)

    wt_a = a_w_in[0].T.astype(bf)
    qt, k, vt = _project(x, norm_gains[0, 0], wt_a, cos_t, sin_t, with_indexer=False)
    attn = _swa_attention(qt, k, vt, a_sinks[0])
    x = _post_attention(attn.reshape(B * S, QD), x.reshape(B * S, D), norm_gains[0],
                        a_w_out[0].astype(bf), w_mlp_up[0].astype(bf), w_mlp_down[0].astype(bf))
    x = x.reshape(B, S, D)

    wb = b_w_in[0].T
    r_qi = QD + 2 * KVD
    r_ki = r_qi + IDX_HEADS * IDX_DIM
    r_wi = r_ki + IDX_DIM
    wt_b = jnp.concatenate([wb[:r_ki], _pad_rows(wb[r_ki:r_wi], 2 * IDX_DIM),
                            _pad_rows(wb[r_wi:], 2 * SUBLANES)], axis=0).astype(bf)
    qt, k, vt, qit, ki, wit = _project(x, norm_gains[1, 0], wt_b, cos_t, sin_t, with_indexer=True)
    attn = _dsa_attention(qt, qit, wit, k, ki, vt)
    x = _post_attention(attn.reshape(B * S, QD), x.reshape(B * S, D), norm_gains[1],
                        b_w_out[0].astype(bf), w_mlp_up[1].astype(bf), w_mlp_down[1].astype(bf))
    return x.reshape(B, S, D)
```

```python
import functools
import math

import jax
import jax.numpy as jnp
from jax import lax
from jax.experimental import pallas as pl
from jax.experimental.pallas import tpu as pltpu

D_MODEL = 1024
HEAD_DIM = 64
N_Q_HEADS = 16
N_KV_HEADS = 4
GROUP = 4
ROT_DIM = 16
ROT_HALF = ROT_DIM // 2
ROPE_THETA = 500000.0
BLOCK = 128
IDX_HEADS = 8
IDX_DIM = 64
TOPK_MAX = 256
D_FF = 4 * D_MODEL
EPS = 1e-6
QD = N_Q_HEADS * HEAD_DIM
KVD = N_KV_HEADS * HEAD_DIM

LANES = 128
SUBLANES = 8
PACKED_ROWS = 16
TOK_TILE = 512
ROW_TILE = 512
FF_CHUNK = 1024
EXP_ROWS = 64
VMEM_LIMIT = 56 * 1024 * 1024
V_ROWS = HEAD_DIM + PACKED_ROWS
INT_MIN = -2 ** 31
NEG_INIT = -1e30
MASK_NEG = -1e9
LOG2E = math.log2(math.e)
MXU_DTYPE = jnp.bfloat16

_NT = (((1,), (1,)), ((), ()))


def _rms(x, g):
    return x * lax.rsqrt(jnp.mean(x * x, axis=-1, keepdims=True) + EPS) * g


def _rope_table_kernel(pos_ref, inv_ref, cos_ref, sin_ref):
    ang = pos_ref[0].astype(jnp.float32) * inv_ref[:, :1]
    cos_ref[0] = jnp.cos(ang)
    sin_ref[0] = jnp.sin(ang)


def _rope_tables(positions):
    B, S = positions.shape
    inv = ROPE_THETA ** (-jnp.arange(0, ROT_DIM, 2, dtype=jnp.float32) / ROT_DIM)
    inv = jnp.broadcast_to(inv[:, None], (ROT_HALF, LANES))
    out = jax.ShapeDtypeStruct((B, ROT_HALF, S), jnp.float32)
    return pl.pallas_call(
        _rope_table_kernel,
        grid=(B,),
        in_specs=[pl.BlockSpec((1, 1, S), lambda b: (b, 0, 0)),
                  pl.BlockSpec((ROT_HALF, LANES), lambda b: (0, 0))],
        out_specs=[pl.BlockSpec((1, ROT_HALF, S), lambda b: (b, 0, 0))] * 2,
        out_shape=[out, out],
        name="rope_tables",
    )(positions.reshape(B, 1, S), inv)


def _rope_heads_t(y, n_heads, cos, sin, scale=None):
    parts = []
    for h in range(n_heads):
        r0 = h * HEAD_DIM
        x1 = y[r0:r0 + ROT_HALF]
        x2 = y[r0 + ROT_HALF:r0 + ROT_DIM]
        parts += [x1 * cos - x2 * sin, x2 * cos + x1 * sin, y[r0 + ROT_DIM:r0 + HEAD_DIM]]
    out = jnp.concatenate(parts, axis=0)
    return out if scale is None else out * scale


def _proj_kernel(x_ref, g_ref, wt_ref, cos_ref, sin_ref, *out_refs, with_indexer):
    hn = _rms(x_ref[0], g_ref[...]).astype(MXU_DTYPE)
    tm = hn.shape[0]
    cos = cos_ref[0]
    sin = sin_ref[0]

    def proj(r0, rows):
        return lax.dot_general(wt_ref[r0:r0 + rows, :], hn, _NT,
                               preferred_element_type=jnp.float32)

    qt_ref, k_ref, vt_ref = out_refs[:3]
    q = proj(0, QD)
    qt_ref[0] = _rope_heads_t(q, N_Q_HEADS, cos, sin, HEAD_DIM ** -0.5 * LOG2E).astype(MXU_DTYPE)
    k = _rope_heads_t(proj(QD, KVD), N_KV_HEADS, cos, sin)
    k_ref[0] = k.T.astype(MXU_DTYPE)
    v = proj(QD + KVD, KVD)
    ones = jnp.ones((PACKED_ROWS, tm), jnp.float32)
    v_rows = []
    for h in range(N_KV_HEADS):
        v_rows += [v[h * HEAD_DIM:(h + 1) * HEAD_DIM], ones]
    vt_ref[0, 0] = jnp.concatenate(v_rows, axis=0).astype(MXU_DTYPE)
    if with_indexer:
        qit_ref, ki_ref, wit_ref = out_refs[3:]
        r0 = QD + 2 * KVD
        qi = _rope_heads_t(proj(r0, IDX_HEADS * IDX_DIM), IDX_HEADS, cos, sin)
        qit_ref[0] = qi.astype(MXU_DTYPE)
        r0 += IDX_HEADS * IDX_DIM
        ki = proj(r0, 2 * IDX_DIM)
        ki = jnp.concatenate([_rope_heads_t(ki[:IDX_DIM], 1, cos, sin), ki[IDX_DIM:]], axis=0)
        ki_ref[0] = ki.T.astype(MXU_DTYPE)
        r0 += 2 * IDX_DIM
        wi = proj(r0, 2 * SUBLANES)
        wit_ref[0] = wi[:IDX_HEADS] * ((IDX_HEADS ** -0.5) * (IDX_DIM ** -0.5))


def _project(x, g, wt, cos_t, sin_t, with_indexer):
    B, S, D = x.shape
    tm = TOK_TILE
    n_rows = wt.shape[0]
    bf = MXU_DTYPE
    vr = N_KV_HEADS * V_ROWS
    out_shape = [jax.ShapeDtypeStruct((B, QD, S), bf),
                 jax.ShapeDtypeStruct((B, S, KVD), bf),
                 jax.ShapeDtypeStruct((B, S // tm, vr, tm), bf)]
    out_specs = [pl.BlockSpec((1, QD, tm), lambda b, i: (b, 0, i)),
                 pl.BlockSpec((1, tm, KVD), lambda b, i: (b, i, 0)),
                 pl.BlockSpec((1, 1, vr, tm), lambda b, i: (b, i, 0, 0))]
    if with_indexer:
        out_shape += [jax.ShapeDtypeStruct((B, IDX_HEADS * IDX_DIM, S), bf),
                      jax.ShapeDtypeStruct((B, S, 2 * IDX_DIM), bf),
                      jax.ShapeDtypeStruct((B, IDX_HEADS, S), jnp.float32)]
        out_specs += [pl.BlockSpec((1, IDX_HEADS * IDX_DIM, tm), lambda b, i: (b, 0, i)),
                      pl.BlockSpec((1, tm, 2 * IDX_DIM), lambda b, i: (b, i, 0)),
                      pl.BlockSpec((1, IDX_HEADS, tm), lambda b, i: (b, 0, i))]
    return pl.pallas_call(
        functools.partial(_proj_kernel, with_indexer=with_indexer),
        grid=(B, S // tm),
        in_specs=[pl.BlockSpec((1, tm, D), lambda b, i: (b, i, 0)),
                  pl.BlockSpec((1, D), lambda b, i: (0, 0)),
                  pl.BlockSpec((n_rows, D), lambda b, i: (0, 0)),
                  pl.BlockSpec((1, ROT_HALF, tm), lambda b, i: (b, 0, i)),
                  pl.BlockSpec((1, ROT_HALF, tm), lambda b, i: (b, 0, i))],
        out_specs=out_specs,
        out_shape=out_shape,
        compiler_params=pltpu.CompilerParams(
            dimension_semantics=("arbitrary", "arbitrary"), vmem_limit_bytes=VMEM_LIMIT),
        name="proj_indexer" if with_indexer else "proj",
    )(x, g.reshape(1, D), wt, cos_t, sin_t)


def _head_weights(qt, h):
    blocks = [qt[(GROUP * h + g) * HEAD_DIM:(GROUP * h + g + 1) * HEAD_DIM, :] for g in range(GROUP)]
    w = jnp.concatenate(blocks, axis=1)
    z = jnp.zeros_like(w)
    return jnp.concatenate([w, z] if h % 2 == 0 else [z, w], axis=0)


def _col_reduce(x, op):
    rows, n = x.shape
    part = op(x.reshape(rows // SUBLANES, SUBLANES, n), axis=0)
    return op(part, axis=0, keepdims=True)


def _swa_kernel(qt_ref, kp_ref, kc_ref, vp_ref, vc_ref, sink_ref, o_ref):
    n = pl.program_id(1)
    qt = qt_ref[0]
    c = lax.broadcasted_iota(jnp.int32, (2 * BLOCK, BLOCK), 0)
    i = lax.broadcasted_iota(jnp.int32, (2 * BLOCK, BLOCK), 1)
    first_slot = jnp.where(n > 0, 0, BLOCK)
    valid = (c > i) & (c <= i + BLOCK) & (c >= first_slot)
    bias = jnp.where(valid, 0.0, -jnp.inf).astype(jnp.float32)
    bias4 = jnp.concatenate([bias] * GROUP, axis=1)
    outs = []
    for h in range(N_KV_HEADS):
        pair = slice(LANES * (h // 2), LANES * (h // 2 + 1))
        kcat = jnp.concatenate([kp_ref[0, :, pair], kc_ref[0, :, pair]], axis=0)
        s = jnp.dot(kcat, _head_weights(qt, h), preferred_element_type=jnp.float32) + bias4
        sink = sink_ref[:, GROUP * h * BLOCK:GROUP * (h + 1) * BLOCK]
        m = jnp.maximum(_col_reduce(s, jnp.max), sink)
        p = jnp.exp2(s - m)
        denom = _col_reduce(p, jnp.sum) + jnp.exp2(sink - m)
        rows = slice(V_ROWS * h, V_ROWS * h + HEAD_DIM)
        vcat = jnp.concatenate([vp_ref[0, 0, rows, :], vc_ref[0, 0, rows, :]], axis=1)
        o = jnp.dot(vcat, p.astype(MXU_DTYPE), preferred_element_type=jnp.float32) / denom
        outs += [o[:, g * BLOCK:(g + 1) * BLOCK] for g in range(GROUP)]
    o_ref[0] = jnp.concatenate(outs, axis=0).T.astype(MXU_DTYPE)


def _swa_attention(qt, k, vt, sinks):
    B, _, S = qt.shape
    nb = S // BLOCK
    per_tile = TOK_TILE // BLOCK
    vr = N_KV_HEADS * V_ROWS
    sink_row = jnp.repeat(sinks.astype(jnp.float32) * LOG2E, BLOCK).reshape(1, N_Q_HEADS * BLOCK)

    def prev(n):
        return jnp.maximum(n - 1, 0)

    return pl.pallas_call(
        _swa_kernel,
        grid=(B, nb),
        in_specs=[pl.BlockSpec((1, QD, BLOCK), lambda b, n: (b, 0, n)),
                  pl.BlockSpec((1, BLOCK, KVD), lambda b, n: (b, prev(n), 0)),
                  pl.BlockSpec((1, BLOCK, KVD), lambda b, n: (b, n, 0)),
                  pl.BlockSpec((1, 1, vr, BLOCK), lambda b, n: (b, prev(n) // per_tile, 0, prev(n) % per_tile)),
                  pl.BlockSpec((1, 1, vr, BLOCK), lambda b, n: (b, n // per_tile, 0, n % per_tile)),
                  pl.BlockSpec((1, N_Q_HEADS * BLOCK), lambda b, n: (0, 0))],
        out_specs=pl.BlockSpec((1, BLOCK, QD), lambda b, n: (b, n, 0)),
        out_shape=jax.ShapeDtypeStruct((B, S, QD), MXU_DTYPE),
        compiler_params=pltpu.CompilerParams(
            dimension_semantics=("arbitrary", "arbitrary"), vmem_limit_bytes=VMEM_LIMIT),
        name="swa_attention",
    )(qt, k, k, vt, vt, sink_row)


def _bit_planes(words):
    a = list(words)
    j, m = 16, 0x0000FFFF
    while j:
        k = 0
        while k < 32:
            t = (a[k] ^ lax.shift_right_logical(a[k + j], jnp.int32(j))) & m
            a[k] = a[k] ^ t
            a[k + j] = a[k + j] ^ (t << j)
            k = (k + j + 1) & ~j
        j >>= 1
        m ^= (m << j) & 0xFFFFFFFF
    return a


def _popcount_cols(x):
    return _col_reduce(lax.population_count(x), jnp.sum)


def _dsa_kernel(qt_ref, qit_ref, wit_ref, k_ref, ki_ref, vt_ref, o_ref,
                skey_ref, planes_ref, wq_ref, m_ref, acc_ref, tie_ref,
                s_ref, p_ref, cmax_ref, bias_ref, *, n_sel):
    n = pl.program_id(1)
    kc = TOK_TILE
    n_chunks = n // (kc // BLOCK) + 1
    n_keys = skey_ref.shape[0]
    idx_bits = (n_keys - 1).bit_length()
    t_row = n * BLOCK + lax.broadcasted_iota(jnp.int32, (1, BLOCK), 1)

    def key_index(j):
        return j * kc + lax.broadcasted_iota(jnp.int32, (kc, BLOCK), 0)

    @pl.when((pl.program_id(0) == 0) & (n == 0))
    def _():
        planes_ref[...] = jnp.zeros(planes_ref.shape, jnp.int32)

    qt = qt_ref[0]
    r = lax.broadcasted_iota(jnp.int32, (LANES, GROUP * BLOCK), 0)
    c = lax.broadcasted_iota(jnp.int32, (LANES, GROUP * BLOCK), 1)
    ident = jnp.where(r == (c & (BLOCK - 1)), 1.0, 0.0).astype(MXU_DTYPE)
    for h in range(N_KV_HEADS):
        wq_ref[h] = jnp.concatenate([_head_weights(qt, h), ident], axis=0)
    qit = qit_ref[0]
    wqi = jnp.concatenate([qit[h * IDX_DIM:(h + 1) * IDX_DIM, :] for h in range(IDX_HEADS)], axis=1)
    wqi = jnp.concatenate([wqi, jnp.zeros_like(wqi)], axis=0)
    wit = wit_ref[0]

    def score_chunk(j, carry):
        rows = pl.ds(pl.multiple_of(j * kc, kc), kc)
        lg = jnp.dot(ki_ref[0, rows, :], wqi, preferred_element_type=jnp.float32)
        score = jnp.zeros((kc, BLOCK), jnp.float32)
        for h in range(IDX_HEADS):
            score = score + jnp.maximum(lg[:, h * BLOCK:(h + 1) * BLOCK], 0.0) * wit[h:h + 1, :]
        bits = pltpu.bitcast(score, jnp.int32)
        skey = bits ^ ((bits >> 31) & 0x7FFFFFFF)
        skey = jnp.where(bits == INT_MIN, 0, skey)
        skey = jnp.where(key_index(j) <= t_row, skey, INT_MIN)
        skey_ref[rows, :] = skey
        ukey = skey ^ INT_MIN
        group = 32 * SUBLANES
        for g in range(kc // group):
            words = [ukey[g * group + SUBLANES * i:g * group + SUBLANES * (i + 1), :] for i in range(32)]
            row0 = pl.multiple_of((j * (kc // group) + g) * SUBLANES, SUBLANES)
            for p, plane in enumerate(_bit_planes(words)):
                planes_ref[p, pl.ds(row0, SUBLANES), :] = plane
        return carry

    lax.fori_loop(0, n_chunks, score_chunk, 0)

    plane_rows = lax.broadcasted_iota(jnp.int32, (n_keys // 32, BLOCK), 0)
    active = jnp.where(plane_rows < n_chunks * (kc // 32), -1, 0)
    zero_row = jnp.zeros((1, BLOCK), jnp.int32)
    cnt_above, thr_u = zero_row, zero_row
    for p in range(32):
        ones = active & planes_ref[p]
        c1 = _popcount_cols(ones)
        take = (cnt_above + c1) >= n_sel
        active = jnp.where(take, ones, active ^ ones)
        cnt_above = jnp.where(take, cnt_above, cnt_above + c1)
        bit = (1 << (31 - p)) - (1 << 32 if p == 0 else 0)
        thr_u = jnp.where(take, thr_u | bit, thr_u)
    cnt_ge = cnt_above + _popcount_cols(active)
    thr = jnp.maximum(thr_u ^ INT_MIN, INT_MIN + 1)
    excess = (cnt_ge > n_sel) & (thr_u != 0)
    tie_ref[...] = jnp.full((1, BLOCK), 2 ** 30, jnp.int32)

    @pl.when(jnp.max(jnp.where(excess, 1, 0)) > 0)
    def _():
        def count(pred):
            def body(j, acc):
                rows = pl.ds(pl.multiple_of(j * kc, kc), kc)
                hit = jnp.where(pred(skey_ref[rows, :], j), 1, 0)
                return acc + jnp.sum(hit.reshape(kc // SUBLANES, SUBLANES, BLOCK), axis=0)
            acc = lax.fori_loop(0, n_chunks, body, jnp.zeros((SUBLANES, BLOCK), jnp.int32))
            return jnp.sum(acc, axis=0, keepdims=True)

        need = n_sel - count(lambda sk, j: sk > thr)

        def idx_step(b, x):
            cand = x | lax.shift_left(jnp.int32(1), idx_bits - 1 - b)
            below = count(lambda sk, j: (sk == thr) & (key_index(j) < cand))
            return jnp.where(below < need, cand, x)

        x = lax.fori_loop(0, idx_bits, idx_step, zero_row)
        tie_ref[...] = jnp.where(excess, x, 2 ** 30)

    tie_x = tie_ref[...]

    m_ref[...] = jnp.full(m_ref.shape, NEG_INIT, jnp.float32)
    acc_ref[...] = jnp.zeros(acc_ref.shape, jnp.float32)

    def select_bias(j):
        sk = skey_ref[pl.ds(pl.multiple_of(j * kc, kc), kc), :]
        sel = (sk > thr) | ((sk == thr) & (key_index(j) <= tie_x))
        bias_ref[...] = jnp.where(sel, 0.0, MASK_NEG).astype(MXU_DTYPE)

    def scores(j, h, slot):
        rows = pl.ds(pl.multiple_of(j * kc, kc), kc)
        pair = slice(LANES * (h // 2), LANES * (h // 2 + 1))
        lhs = jnp.concatenate([k_ref[0, rows, pair], bias_ref[...]], axis=1)
        s = jnp.dot(lhs, wq_ref[h], preferred_element_type=jnp.float32)
        s_ref[slot] = s
        cmax_ref[slot] = _col_reduce(s, jnp.max)

    def accumulate(j, h, slot):
        m_old = m_ref[h]
        m_new = jnp.maximum(m_old, cmax_ref[slot])
        for r0 in range(0, kc, EXP_ROWS):
            p_ref[slot, r0:r0 + EXP_ROWS, :] = jnp.exp2(
                s_ref[slot, r0:r0 + EXP_ROWS, :] - m_new).astype(MXU_DTYPE)
        vt = vt_ref[0, j, V_ROWS * h:V_ROWS * (h + 1), :]
        acc_ref[h] = jnp.exp2(m_old - m_new) * acc_ref[h] + jnp.dot(
            vt, p_ref[slot], preferred_element_type=jnp.float32)
        m_ref[h] = m_new

    select_bias(0)
    scores(0, 0, 0)

    def attend_chunk(j, carry):
        for h in range(N_KV_HEADS):
            if h + 1 < N_KV_HEADS:
                scores(j, h + 1, (h + 1) % 2)
            else:
                j_next = jnp.minimum(j + 1, n_chunks - 1)
                select_bias(j_next)
                scores(j_next, 0, 0)
            accumulate(j, h, h % 2)
        return carry

    lax.fori_loop(0, n_chunks, attend_chunk, 0)

    outs = []
    for h in range(N_KV_HEADS):
        acc = acc_ref[h]
        o = acc[:HEAD_DIM] / acc[HEAD_DIM:HEAD_DIM + 1]
        outs += [o[:, g * BLOCK:(g + 1) * BLOCK] for g in range(GROUP)]
    o_ref[0] = jnp.concatenate(outs, axis=0).T.astype(MXU_DTYPE)


def _dsa_attention(qt, qit, wit, k, ki, vt):
    B, _, S = qt.shape
    nb = S // BLOCK
    n_sel = min(TOPK_MAX, S // 4)
    return pl.pallas_call(
        functools.partial(_dsa_kernel, n_sel=n_sel),
        grid=(B, nb),
        in_specs=[pl.BlockSpec((1, QD, BLOCK), lambda b, n: (b, 0, n)),
                  pl.BlockSpec((1, IDX_HEADS * IDX_DIM, BLOCK), lambda b, n: (b, 0, n)),
                  pl.BlockSpec((1, IDX_HEADS, BLOCK), lambda b, n: (b, 0, n)),
                  pl.BlockSpec((1, S, KVD), lambda b, n: (b, 0, 0)),
                  pl.BlockSpec((1, S, 2 * IDX_DIM), lambda b, n: (b, 0, 0)),
                  pl.BlockSpec((1, S // TOK_TILE, N_KV_HEADS * V_ROWS, TOK_TILE), lambda b, n: (b, 0, 0, 0))],
        out_specs=pl.BlockSpec((1, BLOCK, QD), lambda b, n: (b, n, 0)),
        out_shape=jax.ShapeDtypeStruct((B, S, QD), MXU_DTYPE),
        scratch_shapes=[pltpu.VMEM((S, BLOCK), jnp.int32),
                        pltpu.VMEM((32, S // 32, BLOCK), jnp.int32),
                        pltpu.VMEM((N_KV_HEADS, 2 * LANES, GROUP * BLOCK), MXU_DTYPE),
                        pltpu.VMEM((N_KV_HEADS, 1, GROUP * BLOCK), jnp.float32),
                        pltpu.VMEM((N_KV_HEADS, V_ROWS, GROUP * BLOCK), jnp.float32),
                        pltpu.VMEM((1, BLOCK), jnp.int32),
                        pltpu.VMEM((2, TOK_TILE, GROUP * BLOCK), jnp.float32),
                        pltpu.VMEM((2, TOK_TILE, GROUP * BLOCK), MXU_DTYPE),
                        pltpu.VMEM((2, 1, GROUP * BLOCK), jnp.float32),
                        pltpu.VMEM((TOK_TILE, BLOCK), MXU_DTYPE)],
        compiler_params=pltpu.CompilerParams(
            dimension_semantics=("arbitrary", "arbitrary"), vmem_limit_bytes=VMEM_LIMIT),
        name="dsa_attention",
    )(qt, qit, wit, k, ki, vt)


def _post_kernel(a_ref, x_ref, g_ref, wo_ref, wu_ref, wd_ref, o_ref):
    mix = jnp.dot(a_ref[...], wo_ref[...], preferred_element_type=jnp.float32)
    x1 = x_ref[...] + _rms(mix, g_ref[1:2, :])
    hn = _rms(x1, g_ref[2:3, :]).astype(MXU_DTYPE)
    acc = jnp.zeros(x1.shape, jnp.float32)
    for c in range(D_FF // FF_CHUNK):
        cols = slice(c * FF_CHUNK, (c + 1) * FF_CHUNK)
        a = jnp.maximum(jnp.dot(hn, wu_ref[:, cols], preferred_element_type=jnp.float32), 0.0)
        acc = acc + jnp.dot((a * a).astype(MXU_DTYPE), wd_ref[cols, :],
                            preferred_element_type=jnp.float32)
    o_ref[...] = x1 + _rms(acc, g_ref[3:4, :])


def _post_attention(attn, x, g, w_out, w_up, w_down):
    T, D = x.shape
    tm = ROW_TILE
    return pl.pallas_call(
        _post_kernel,
        grid=(T // tm,),
        in_specs=[pl.BlockSpec((tm, QD), lambda i: (i, 0)),
                  pl.BlockSpec((tm, D), lambda i: (i, 0)),
                  pl.BlockSpec((4, D), lambda i: (0, 0)),
                  pl.BlockSpec((QD, D), lambda i: (0, 0), pipeline_mode=pl.Buffered(1)),
                  pl.BlockSpec((D, D_FF), lambda i: (0, 0), pipeline_mode=pl.Buffered(1)),
                  pl.BlockSpec((D_FF, D), lambda i: (0, 0), pipeline_mode=pl.Buffered(1))],
        out_specs=pl.BlockSpec((tm, D), lambda i: (i, 0)),
        out_shape=jax.ShapeDtypeStruct((T, D), jnp.float32),
        compiler_params=pltpu.CompilerParams(
            dimension_semantics=("arbitrary",), vmem_limit_bytes=VMEM_LIMIT),
        name="post_attention",
    )(attn, x, g, w_out, w_up, w_down)


def _pad_rows(w, rows):
    return jnp.concatenate([w, jnp.zeros((rows - w.shape[0], w.shape[1]), w.dtype)], axis=0)


def kernel(x, positions, norm_gains, w_mlp_up, w_mlp_down, a_w_in, a_w_out, a_sinks, b_w_in, b_w_out):
    B, S, D = x.shape
    bf = MXU_DTYPE
    cos_t, sin_t = _rope_tables(positions)

    wt_a = a_w_in[0].T.astype(bf)
    qt, k, vt = _project(x, norm_gains[0, 0], wt_a, cos_t, sin_t, with_indexer=False)
    attn = _swa_attention(qt, k, vt, a_sinks[0])
    x = _post_attention(attn.reshape(B * S, QD), x.reshape(B * S, D), norm_gains[0],
                        a_w_out[0].astype(bf), w_mlp_up[0].astype(bf), w_mlp_down[0].astype(bf))
    x = x.reshape(B, S, D)

    wb = b_w_in[0].T
    r_qi = QD + 2 * KVD
    r_ki = r_qi + IDX_HEADS * IDX_DIM
    r_wi = r_ki + IDX_DIM
    wt_b = jnp.concatenate([wb[:r_ki], _pad_rows(wb[r_ki:r_wi], 2 * IDX_DIM),
                            _pad_rows(wb[r_wi:], 2 * SUBLANES)], axis=0).astype(bf)
    qt, k, vt, qit, ki, wit = _project(x, norm_gains[1, 0], wt_b, cos_t, sin_t, with_indexer=True)
    attn = _dsa_attention(qt, qit, wit, k, ki, vt)
    x = _post_attention(attn.reshape(B * S, QD), x.reshape(B * S, D), norm_gains[1],
                        b_w_out[0].astype(bf), w_mlp_up[1].astype(bf), w_mlp_down[1].astype(bf))
    return x.reshape(B, S, D)
```

```python
import functools
import math

import jax
import jax.numpy as jnp
from jax import lax
from jax.experimental import pallas as pl
from jax.experimental.pallas import tpu as pltpu

D_MODEL = 1024
HEAD_DIM = 64
N_Q_HEADS = 16
N_KV_HEADS = 4
GROUP = 4
ROT_DIM = 16
ROT_HALF = ROT_DIM // 2
ROPE_THETA = 500000.0
BLOCK = 128
IDX_HEADS = 8
IDX_DIM = 64
TOPK_MAX = 256
D_FF = 4 * D_MODEL
EPS = 1e-6
QD = N_Q_HEADS * HEAD_DIM
KVD = N_KV_HEADS * HEAD_DIM

LANES = 128
SUBLANES = 8
PACKED_ROWS = 16
TOK_TILE = 512
ROW_TILE = 512
FF_CHUNK = 1024
EXP_ROWS = 64
VMEM_LIMIT = 56 * 1024 * 1024
V_ROWS = HEAD_DIM + PACKED_ROWS
INT_MIN = -2 ** 31
NEG_INIT = -1e30
MASK_NEG = -1e9
LOG2E = math.log2(math.e)
MXU_DTYPE = jnp.bfloat16

_NT = (((1,), (1,)), ((), ()))


def _rms(x, g):
    return x * lax.rsqrt(jnp.mean(x * x, axis=-1, keepdims=True) + EPS) * g


def _rope_table_kernel(pos_ref, inv_ref, cos_ref, sin_ref):
    ang = pos_ref[0].astype(jnp.float32) * inv_ref[:, :1]
    cos_ref[0] = jnp.cos(ang)
    sin_ref[0] = jnp.sin(ang)


def _rope_tables(positions):
    B, S = positions.shape
    inv = ROPE_THETA ** (-jnp.arange(0, ROT_DIM, 2, dtype=jnp.float32) / ROT_DIM)
    inv = jnp.broadcast_to(inv[:, None], (ROT_HALF, LANES))
    out = jax.ShapeDtypeStruct((B, ROT_HALF, S), jnp.float32)
    return pl.pallas_call(
        _rope_table_kernel,
        grid=(B,),
        in_specs=[pl.BlockSpec((1, 1, S), lambda b: (b, 0, 0)),
                  pl.BlockSpec((ROT_HALF, LANES), lambda b: (0, 0))],
        out_specs=[pl.BlockSpec((1, ROT_HALF, S), lambda b: (b, 0, 0))] * 2,
        out_shape=[out, out],
        name="rope_tables",
    )(positions.reshape(B, 1, S), inv)


def _rope_heads_t(y, n_heads, cos, sin, scale=None):
    parts = []
    for h in range(n_heads):
        r0 = h * HEAD_DIM
        x1 = y[r0:r0 + ROT_HALF]
        x2 = y[r0 + ROT_HALF:r0 + ROT_DIM]
        parts += [x1 * cos - x2 * sin, x2 * cos + x1 * sin, y[r0 + ROT_DIM:r0 + HEAD_DIM]]
    out = jnp.concatenate(parts, axis=0)
    return out if scale is None else out * scale


def _proj_kernel(x_ref, g_ref, wt_ref, cos_ref, sin_ref, *out_refs, with_indexer):
    hn = _rms(x_ref[0], g_ref[...]).astype(MXU_DTYPE)
    tm = hn.shape[0]
    cos = cos_ref[0]
    sin = sin_ref[0]

    def proj(r0, rows):
        return lax.dot_general(wt_ref[r0:r0 + rows, :], hn, _NT,
                               preferred_element_type=jnp.float32)

    qt_ref, k_ref, vt_ref = out_refs[:3]
    q = proj(0, QD)
    qt_ref[0] = _rope_heads_t(q, N_Q_HEADS, cos, sin, HEAD_DIM ** -0.5 * LOG2E).astype(MXU_DTYPE)
    k = _rope_heads_t(proj(QD, KVD), N_KV_HEADS, cos, sin)
    k_ref[0] = k.T.astype(MXU_DTYPE)
    v = proj(QD + KVD, KVD)
    ones = jnp.ones((PACKED_ROWS, tm), jnp.float32)
    v_rows = []
    for h in range(N_KV_HEADS):
        v_rows += [v[h * HEAD_DIM:(h + 1) * HEAD_DIM], ones]
    vt_ref[0, 0] = jnp.concatenate(v_rows, axis=0).astype(MXU_DTYPE)
    if with_indexer:
        qit_ref, ki_ref, wit_ref = out_refs[3:]
        r0 = QD + 2 * KVD
        qi = _rope_heads_t(proj(r0, IDX_HEADS * IDX_DIM), IDX_HEADS, cos, sin)
        qit_ref[0] = qi.astype(MXU_DTYPE)
        r0 += IDX_HEADS * IDX_DIM
        ki = proj(r0, 2 * IDX_DIM)
        ki = jnp.concatenate([_rope_heads_t(ki[:IDX_DIM], 1, cos, sin), ki[IDX_DIM:]], axis=0)
        ki_ref[0] = ki.T.astype(MXU_DTYPE)
        r0 += 2 * IDX_DIM
        wi = proj(r0, 2 * SUBLANES)
        wit_ref[0] = wi[:IDX_HEADS] * ((IDX_HEADS ** -0.5) * (IDX_DIM ** -0.5))


def _project(x, g, wt, cos_t, sin_t, with_indexer):
    B, S, D = x.shape
    tm = TOK_TILE
    n_rows = wt.shape[0]
    bf = MXU_DTYPE
    vr = N_KV_HEADS * V_ROWS
    out_shape = [jax.ShapeDtypeStruct((B, QD, S), bf),
                 jax.ShapeDtypeStruct((B, S, KVD), bf),
                 jax.ShapeDtypeStruct((B, S // tm, vr, tm), bf)]
    out_specs = [pl.BlockSpec((1, QD, tm), lambda b, i: (b, 0, i)),
                 pl.BlockSpec((1, tm, KVD), lambda b, i: (b, i, 0)),
                 pl.BlockSpec((1, 1, vr, tm), lambda b, i: (b, i, 0, 0))]
    if with_indexer:
        out_shape += [jax.ShapeDtypeStruct((B, IDX_HEADS * IDX_DIM, S), bf),
                      jax.ShapeDtypeStruct((B, S, 2 * IDX_DIM), bf),
                      jax.ShapeDtypeStruct((B, IDX_HEADS, S), jnp.float32)]
        out_specs += [pl.BlockSpec((1, IDX_HEADS * IDX_DIM, tm), lambda b, i: (b, 0, i)),
                      pl.BlockSpec((1, tm, 2 * IDX_DIM), lambda b, i: (b, i, 0)),
                      pl.BlockSpec((1, IDX_HEADS, tm), lambda b, i: (b, 0, i))]
    return pl.pallas_call(
        functools.partial(_proj_kernel, with_indexer=with_indexer),
        grid=(B, S // tm),
        in_specs=[pl.BlockSpec((1, tm, D), lambda b, i: (b, i, 0)),
                  pl.BlockSpec((1, D), lambda b, i: (0, 0)),
                  pl.BlockSpec((n_rows, D), lambda b, i: (0, 0)),
                  pl.BlockSpec((1, ROT_HALF, tm), lambda b, i: (b, 0, i)),
                  pl.BlockSpec((1, ROT_HALF, tm), lambda b, i: (b, 0, i))],
        out_specs=out_specs,
        out_shape=out_shape,
        compiler_params=pltpu.CompilerParams(
            dimension_semantics=("arbitrary", "arbitrary"), vmem_limit_bytes=VMEM_LIMIT),
        name="proj_indexer" if with_indexer else "proj",
    )(x, g.reshape(1, D), wt, cos_t, sin_t)


def _head_weights(qt, h):
    blocks = [qt[(GROUP * h + g) * HEAD_DIM:(GROUP * h + g + 1) * HEAD_DIM, :] for g in range(GROUP)]
    w = jnp.concatenate(blocks, axis=1)
    z = jnp.zeros_like(w)
    return jnp.concatenate([w, z] if h % 2 == 0 else [z, w], axis=0)


def _bias_identity():
    r = lax.broadcasted_iota(jnp.int32, (LANES, GROUP * BLOCK), 0)
    c = lax.broadcasted_iota(jnp.int32, (LANES, GROUP * BLOCK), 1)
    return jnp.where(r == (c & (BLOCK - 1)), 1.0, 0.0).astype(MXU_DTYPE)


def _col_reduce(x, op):
    rows, n = x.shape
    part = op(x.reshape(rows // SUBLANES, SUBLANES, n), axis=0)
    return op(part, axis=0, keepdims=True)


def _swa_kernel(qt_ref, kp_ref, kc_ref, vp_ref, vc_ref, sink_ref, o_ref, s_ref):
    n = pl.program_id(1)
    qt = qt_ref[0]
    c = lax.broadcasted_iota(jnp.int32, (2 * BLOCK, BLOCK), 0)
    i = lax.broadcasted_iota(jnp.int32, (2 * BLOCK, BLOCK), 1)
    first_slot = jnp.where(n > 0, 0, BLOCK)
    valid = (c > i) & (c <= i + BLOCK) & (c >= first_slot)
    bias = jnp.where(valid, 0.0, MASK_NEG).astype(MXU_DTYPE)
    ident = _bias_identity()

    def scores(h):
        pair = slice(LANES * (h // 2), LANES * (h // 2 + 1))
        kcat = jnp.concatenate([kp_ref[0, :, pair], kc_ref[0, :, pair]], axis=0)
        w = jnp.concatenate([_head_weights(qt, h), ident], axis=0)
        s_ref[h % 2] = jnp.dot(jnp.concatenate([kcat, bias], axis=1), w,
                               preferred_element_type=jnp.float32)

    def finish(h):
        s = s_ref[h % 2]
        sink = sink_ref[:, GROUP * h * BLOCK:GROUP * (h + 1) * BLOCK]
        m = jnp.maximum(_col_reduce(s, jnp.max), sink)
        p = jnp.exp2(s - m).astype(MXU_DTYPE)
        rows = slice(V_ROWS * h, V_ROWS * (h + 1))
        vcat = jnp.concatenate([vp_ref[0, 0, rows, :], vc_ref[0, 0, rows, :]], axis=1)
        o = jnp.dot(vcat, p, preferred_element_type=jnp.float32)
        o = o[:HEAD_DIM] / (o[HEAD_DIM:HEAD_DIM + 1] + jnp.exp2(sink - m))
        return [o[:, g * BLOCK:(g + 1) * BLOCK] for g in range(GROUP)]

    outs = []
    scores(0)
    for h in range(N_KV_HEADS):
        if h + 1 < N_KV_HEADS:
            scores(h + 1)
        outs += finish(h)
    o_ref[0] = jnp.concatenate(outs, axis=0).T.astype(MXU_DTYPE)


def _swa_attention(qt, k, vt, sinks):
    B, _, S = qt.shape
    nb = S // BLOCK
    per_tile = TOK_TILE // BLOCK
    vr = N_KV_HEADS * V_ROWS
    sink_row = jnp.repeat(sinks.astype(jnp.float32) * LOG2E, BLOCK).reshape(1, N_Q_HEADS * BLOCK)

    def prev(n):
        return jnp.maximum(n - 1, 0)

    return pl.pallas_call(
        _swa_kernel,
        grid=(B, nb),
        in_specs=[pl.BlockSpec((1, QD, BLOCK), lambda b, n: (b, 0, n)),
                  pl.BlockSpec((1, BLOCK, KVD), lambda b, n: (b, prev(n), 0)),
                  pl.BlockSpec((1, BLOCK, KVD), lambda b, n: (b, n, 0)),
                  pl.BlockSpec((1, 1, vr, BLOCK), lambda b, n: (b, prev(n) // per_tile, 0, prev(n) % per_tile)),
                  pl.BlockSpec((1, 1, vr, BLOCK), lambda b, n: (b, n // per_tile, 0, n % per_tile)),
                  pl.BlockSpec((1, N_Q_HEADS * BLOCK), lambda b, n: (0, 0))],
        out_specs=pl.BlockSpec((1, BLOCK, QD), lambda b, n: (b, n, 0)),
        out_shape=jax.ShapeDtypeStruct((B, S, QD), MXU_DTYPE),
        scratch_shapes=[pltpu.VMEM((2, 2 * BLOCK, GROUP * BLOCK), jnp.float32)],
        compiler_params=pltpu.CompilerParams(
            dimension_semantics=("arbitrary", "arbitrary"), vmem_limit_bytes=VMEM_LIMIT),
        name="swa_attention",
    )(qt, k, k, vt, vt, sink_row)


def _bit_planes(words):
    a = list(words)
    j, m = 16, 0x0000FFFF
    while j:
        k = 0
        while k < 32:
            t = (a[k] ^ lax.shift_right_logical(a[k + j], jnp.int32(j))) & m
            a[k] = a[k] ^ t
            a[k + j] = a[k + j] ^ (t << j)
            k = (k + j + 1) & ~j
        j >>= 1
        m ^= (m << j) & 0xFFFFFFFF
    return a


def _popcount_cols(x):
    return _col_reduce(lax.population_count(x), jnp.sum)


def _dsa_kernel(qt_ref, qit_ref, wit_ref, k_ref, ki_ref, vt_ref, o_ref,
                skey_ref, planes_ref, wq_ref, m_ref, acc_ref, tie_ref,
                s_ref, p_ref, cmax_ref, bias_ref, lg_ref, *, n_sel):
    n = pl.program_id(1)
    kc = TOK_TILE
    n_chunks = n // (kc // BLOCK) + 1
    n_keys = skey_ref.shape[0]
    idx_bits = (n_keys - 1).bit_length()
    t_row = n * BLOCK + lax.broadcasted_iota(jnp.int32, (1, BLOCK), 1)

    def key_index(j):
        return j * kc + lax.broadcasted_iota(jnp.int32, (kc, BLOCK), 0)

    @pl.when((pl.program_id(0) == 0) & (n == 0))
    def _():
        planes_ref[...] = jnp.zeros(planes_ref.shape, jnp.int32)

    qt = qt_ref[0]
    ident = _bias_identity()
    for h in range(N_KV_HEADS):
        wq_ref[h] = jnp.concatenate([_head_weights(qt, h), ident], axis=0)
    qit = qit_ref[0]
    wqi = jnp.concatenate([qit[h * IDX_DIM:(h + 1) * IDX_DIM, :] for h in range(IDX_HEADS)], axis=1)
    wqi = jnp.concatenate([wqi, jnp.zeros_like(wqi)], axis=0)
    wit = wit_ref[0]

    half = IDX_HEADS // 2

    def index_logits(j, slot):
        rows = pl.ds(pl.multiple_of(j * kc, kc), kc)
        cols = slice(slot * half * BLOCK, (slot + 1) * half * BLOCK)
        lg_ref[slot] = jnp.dot(ki_ref[0, rows, :], wqi[:, cols], preferred_element_type=jnp.float32)

    def weighted_relu(slot):
        lg = lg_ref[slot]
        out = None
        for h in range(half):
            term = jnp.maximum(lg[:, h * BLOCK:(h + 1) * BLOCK], 0.0) * wit[slot * half + h:slot * half + h + 1, :]
            out = term if out is None else out + term
        return out

    index_logits(0, 0)

    def score_chunk(j, carry):
        rows = pl.ds(pl.multiple_of(j * kc, kc), kc)
        index_logits(j, 1)
        score = weighted_relu(0)
        index_logits(jnp.minimum(j + 1, n_chunks - 1), 0)
        score = score + weighted_relu(1)
        bits = pltpu.bitcast(score, jnp.int32)
        skey = bits ^ ((bits >> 31) & 0x7FFFFFFF)
        skey = jnp.where(bits == INT_MIN, 0, skey)
        skey = jnp.where(key_index(j) <= t_row, skey, INT_MIN)
        skey_ref[rows, :] = skey
        ukey = skey ^ INT_MIN
        group = 32 * SUBLANES
        for g in range(kc // group):
            words = [ukey[g * group + SUBLANES * i:g * group + SUBLANES * (i + 1), :] for i in range(32)]
            row0 = pl.multiple_of((j * (kc // group) + g) * SUBLANES, SUBLANES)
            for p, plane in enumerate(_bit_planes(words)):
                planes_ref[p, pl.ds(row0, SUBLANES), :] = plane
        return carry

    lax.fori_loop(0, n_chunks, score_chunk, 0)

    plane_rows = lax.broadcasted_iota(jnp.int32, (n_keys // 32, BLOCK), 0)
    active = jnp.where(plane_rows < n_chunks * (kc // 32), -1, 0)
    zero_row = jnp.zeros((1, BLOCK), jnp.int32)
    cnt_above, thr_u = zero_row, zero_row
    for p in range(32):
        ones = active & planes_ref[p]
        c1 = _popcount_cols(ones)
        take = (cnt_above + c1) >= n_sel
        active = jnp.where(take, ones, active ^ ones)
        cnt_above = jnp.where(take, cnt_above, cnt_above + c1)
        bit = (1 << (31 - p)) - (1 << 32 if p == 0 else 0)
        thr_u = jnp.where(take, thr_u | bit, thr_u)
    cnt_ge = cnt_above + _popcount_cols(active)
    thr = jnp.maximum(thr_u ^ INT_MIN, INT_MIN + 1)
    excess = (cnt_ge > n_sel) & (thr_u != 0)
    tie_ref[...] = jnp.full((1, BLOCK), 2 ** 30, jnp.int32)

    @pl.when(jnp.max(jnp.where(excess, 1, 0)) > 0)
    def _():
        def count(pred):
            def body(j, acc):
                rows = pl.ds(pl.multiple_of(j * kc, kc), kc)
                hit = jnp.where(pred(skey_ref[rows, :], j), 1, 0)
                return acc + jnp.sum(hit.reshape(kc // SUBLANES, SUBLANES, BLOCK), axis=0)
            acc = lax.fori_loop(0, n_chunks, body, jnp.zeros((SUBLANES, BLOCK), jnp.int32))
            return jnp.sum(acc, axis=0, keepdims=True)

        need = n_sel - count(lambda sk, j: sk > thr)

        def idx_step(b, x):
            cand = x | lax.shift_left(jnp.int32(1), idx_bits - 1 - b)
            below = count(lambda sk, j: (sk == thr) & (key_index(j) < cand))
            return jnp.where(below < need, cand, x)

        x = lax.fori_loop(0, idx_bits, idx_step, zero_row)
        tie_ref[...] = jnp.where(excess, x, 2 ** 30)

    tie_x = tie_ref[...]

    m_ref[...] = jnp.full(m_ref.shape, NEG_INIT, jnp.float32)
    acc_ref[...] = jnp.zeros(acc_ref.shape, jnp.float32)

    def select_bias(j):
        sk = skey_ref[pl.ds(pl.multiple_of(j * kc, kc), kc), :]
        sel = (sk > thr) | ((sk == thr) & (key_index(j) <= tie_x))
        bias_ref[...] = jnp.where(sel, 0.0, MASK_NEG).astype(MXU_DTYPE)

    def scores(j, h, slot):
        rows = pl.ds(pl.multiple_of(j * kc, kc), kc)
        pair = slice(LANES * (h // 2), LANES * (h // 2 + 1))
        lhs = jnp.concatenate([k_ref[0, rows, pair], bias_ref[...]], axis=1)
        s = jnp.dot(lhs, wq_ref[h], preferred_element_type=jnp.float32)
        s_ref[slot] = s
        cmax_ref[slot] = _col_reduce(s, jnp.max)

    def accumulate(j, h, slot):
        m_old = m_ref[h]
        m_new = jnp.maximum(m_old, cmax_ref[slot])
        for r0 in range(0, kc, EXP_ROWS):
            p_ref[slot, r0:r0 + EXP_ROWS, :] = jnp.exp2(
                s_ref[slot, r0:r0 + EXP_ROWS, :] - m_new).astype(MXU_DTYPE)
        vt = vt_ref[0, j, V_ROWS * h:V_ROWS * (h + 1), :]
        acc_ref[h] = jnp.exp2(m_old - m_new) * acc_ref[h] + jnp.dot(
            vt, p_ref[slot], preferred_element_type=jnp.float32)
        m_ref[h] = m_new

    select_bias(0)
    scores(0, 0, 0)

    def attend_chunk(j, carry):
        for h in range(N_KV_HEADS):
            if h + 1 < N_KV_HEADS:
                scores(j, h + 1, (h + 1) % 2)
            else:
                j_next = jnp.minimum(j + 1, n_chunks - 1)
                select_bias(j_next)
                scores(j_next, 0, 0)
            accumulate(j, h, h % 2)
        return carry

    lax.fori_loop(0, n_chunks, attend_chunk, 0)

    outs = []
    for h in range(N_KV_HEADS):
        acc = acc_ref[h]
        o = acc[:HEAD_DIM] / acc[HEAD_DIM:HEAD_DIM + 1]
        outs += [o[:, g * BLOCK:(g + 1) * BLOCK] for g in range(GROUP)]
    o_ref[0] = jnp.concatenate(outs, axis=0).T.astype(MXU_DTYPE)


def _dsa_attention(qt, qit, wit, k, ki, vt):
    B, _, S = qt.shape
    nb = S // BLOCK
    n_sel = min(TOPK_MAX, S // 4)
    return pl.pallas_call(
        functools.partial(_dsa_kernel, n_sel=n_sel),
        grid=(B, nb),
        in_specs=[pl.BlockSpec((1, QD, BLOCK), lambda b, n: (b, 0, n)),
                  pl.BlockSpec((1, IDX_HEADS * IDX_DIM, BLOCK), lambda b, n: (b, 0, n)),
                  pl.BlockSpec((1, IDX_HEADS, BLOCK), lambda b, n: (b, 0, n)),
                  pl.BlockSpec((1, S, KVD), lambda b, n: (b, 0, 0)),
                  pl.BlockSpec((1, S, 2 * IDX_DIM), lambda b, n: (b, 0, 0)),
                  pl.BlockSpec((1, S // TOK_TILE, N_KV_HEADS * V_ROWS, TOK_TILE), lambda b, n: (b, 0, 0, 0))],
        out_specs=pl.BlockSpec((1, BLOCK, QD), lambda b, n: (b, n, 0)),
        out_shape=jax.ShapeDtypeStruct((B, S, QD), MXU_DTYPE),
        scratch_shapes=[pltpu.VMEM((S, BLOCK), jnp.int32),
                        pltpu.VMEM((32, S // 32, BLOCK), jnp.int32),
                        pltpu.VMEM((N_KV_HEADS, 2 * LANES, GROUP * BLOCK), MXU_DTYPE),
                        pltpu.VMEM((N_KV_HEADS, 1, GROUP * BLOCK), jnp.float32),
                        pltpu.VMEM((N_KV_HEADS, V_ROWS, GROUP * BLOCK), jnp.float32),
                        pltpu.VMEM((1, BLOCK), jnp.int32),
                        pltpu.VMEM((2, TOK_TILE, GROUP * BLOCK), jnp.float32),
                        pltpu.VMEM((2, TOK_TILE, GROUP * BLOCK), MXU_DTYPE),
                        pltpu.VMEM((2, 1, GROUP * BLOCK), jnp.float32),
                        pltpu.VMEM((TOK_TILE, BLOCK), MXU_DTYPE),
                        pltpu.VMEM((2, TOK_TILE, IDX_HEADS // 2 * BLOCK), jnp.float32)],
        compiler_params=pltpu.CompilerParams(
            dimension_semantics=("arbitrary", "arbitrary"), vmem_limit_bytes=VMEM_LIMIT),
        name="dsa_attention",
    )(qt, qit, wit, k, ki, vt)


def _post_kernel(a_ref, x_ref, g_ref, wo_ref, wu_ref, wd_ref, o_ref):
    mix = jnp.dot(a_ref[...], wo_ref[...], preferred_element_type=jnp.float32)
    x1 = x_ref[...] + _rms(mix, g_ref[1:2, :])
    hn = _rms(x1, g_ref[2:3, :]).astype(MXU_DTYPE)
    acc = jnp.zeros(x1.shape, jnp.float32)
    for c in range(D_FF // FF_CHUNK):
        cols = slice(c * FF_CHUNK, (c + 1) * FF_CHUNK)
        a = jnp.maximum(jnp.dot(hn, wu_ref[:, cols], preferred_element_type=jnp.float32), 0.0)
        acc = acc + jnp.dot((a * a).astype(MXU_DTYPE), wd_ref[cols, :],
                            preferred_element_type=jnp.float32)
    o_ref[...] = x1 + _rms(acc, g_ref[3:4, :])


def _post_attention(attn, x, g, w_out, w_up, w_down):
    T, D = x.shape
    tm = ROW_TILE
    return pl.pallas_call(
        _post_kernel,
        grid=(T // tm,),
        in_specs=[pl.BlockSpec((tm, QD), lambda i: (i, 0)),
                  pl.BlockSpec((tm, D), lambda i: (i, 0)),
                  pl.BlockSpec((4, D), lambda i: (0, 0)),
                  pl.BlockSpec((QD, D), lambda i: (0, 0), pipeline_mode=pl.Buffered(1)),
                  pl.BlockSpec((D, D_FF), lambda i: (0, 0), pipeline_mode=pl.Buffered(1)),
                  pl.BlockSpec((D_FF, D), lambda i: (0, 0), pipeline_mode=pl.Buffered(1))],
        out_specs=pl.BlockSpec((tm, D), lambda i: (i, 0)),
        out_shape=jax.ShapeDtypeStruct((T, D), jnp.float32),
        compiler_params=pltpu.CompilerParams(
            dimension_semantics=("arbitrary",), vmem_limit_bytes=VMEM_LIMIT),
        name="post_attention",
    )(attn, x, g, w_out, w_up, w_down)


def _pad_rows(w, rows):
    return jnp.concatenate([w, jnp.zeros((rows - w.shape[0], w.shape[1]), w.dtype)], axis=0)


def kernel(x, positions, norm_gains, w_mlp_up, w_mlp_down, a_w_in, a_w_out, a_sinks, b_w_in, b_w_out):
    B, S, D = x.shape
    bf = MXU_DTYPE
    cos_t, sin_t = _rope_tables(positions)

    wt_a = a_w_in[0].T.astype(bf)
    qt, k, vt = _project(x, norm_gains[0, 0], wt_a, cos_t, sin_t, with_indexer=False)
    attn = _swa_attention(qt, k, vt, a_sinks[0])
    x = _post_attention(attn.reshape(B * S, QD), x.reshape(B * S, D), norm_gains[0],
                        a_w_out[0].astype(bf), w_mlp_up[0].astype(bf), w_mlp_down[0].astype(bf))
    x = x.reshape(B, S, D)

    wb = b_w_in[0].T
    r_qi = QD + 2 * KVD
    r_ki = r_qi + IDX_HEADS * IDX_DIM
    r_wi = r_ki + IDX_DIM
    wt_b = jnp.concatenate([wb[:r_ki], _pad_rows(wb[r_ki:r_wi], 2 * IDX_DIM),
                            _pad_rows(wb[r_wi:], 2 * SUBLANES)], axis=0).astype(bf)
    qt, k, vt, qit, ki, wit = _project(x, norm_gains[1, 0], wt_b, cos_t, sin_t, with_indexer=True)
    attn = _dsa_attention(qt, qit, wit, k, ki, vt)
    x = _post_attention(attn.reshape(B * S, QD), x.reshape(B * S, D), norm_gains[1],
                        b_w_out[0].astype(bf), w_mlp_up[1].astype(bf), w_mlp_down[1].astype(bf))
    return x.reshape(B, S, D)
```

```python
import functools
import math

import jax
import jax.numpy as jnp
from jax import lax
from jax.experimental import pallas as pl
from jax.experimental.pallas import tpu as pltpu

D_MODEL = 1024
HEAD_DIM = 64
N_Q_HEADS = 16
N_KV_HEADS = 4
GROUP = 4
ROT_DIM = 16
ROT_HALF = ROT_DIM // 2
ROPE_THETA = 500000.0
BLOCK = 128
IDX_HEADS = 8
IDX_DIM = 64
TOPK_MAX = 256
D_FF = 4 * D_MODEL
EPS = 1e-6
QD = N_Q_HEADS * HEAD_DIM
KVD = N_KV_HEADS * HEAD_DIM

LANES = 128
SUBLANES = 8
PACKED_ROWS = 16
TOK_TILE = 512
ROW_TILE = 512
FF_CHUNK = 1024
EXP_ROWS = 64
VMEM_LIMIT = 56 * 1024 * 1024
V_ROWS = HEAD_DIM + PACKED_ROWS
INT_MIN = -2 ** 31
NEG_INIT = -1e30
MASK_NEG = -1e9
LOG2E = math.log2(math.e)
MXU_DTYPE = jnp.bfloat16

_NT = (((1,), (1,)), ((), ()))


def _rms(x, g):
    return x * lax.rsqrt(jnp.mean(x * x, axis=-1, keepdims=True) + EPS) * g


def _rope_table_kernel(pos_ref, inv_ref, cos_ref, sin_ref):
    ang = pos_ref[0].astype(jnp.float32) * inv_ref[:, :1]
    cos_ref[0] = jnp.cos(ang)
    sin_ref[0] = jnp.sin(ang)


def _rope_tables(positions):
    B, S = positions.shape
    inv = ROPE_THETA ** (-jnp.arange(0, ROT_DIM, 2, dtype=jnp.float32) / ROT_DIM)
    inv = jnp.broadcast_to(inv[:, None], (ROT_HALF, LANES))
    out = jax.ShapeDtypeStruct((B, ROT_HALF, S), jnp.float32)
    return pl.pallas_call(
        _rope_table_kernel,
        grid=(B,),
        in_specs=[pl.BlockSpec((1, 1, S), lambda b: (b, 0, 0)),
                  pl.BlockSpec((ROT_HALF, LANES), lambda b: (0, 0))],
        out_specs=[pl.BlockSpec((1, ROT_HALF, S), lambda b: (b, 0, 0))] * 2,
        out_shape=[out, out],
        name="rope_tables",
    )(positions.reshape(B, 1, S), inv)


def _rope_heads_t(y, n_heads, cos, sin, scale=None):
    parts = []
    for h in range(n_heads):
        r0 = h * HEAD_DIM
        x1 = y[r0:r0 + ROT_HALF]
        x2 = y[r0 + ROT_HALF:r0 + ROT_DIM]
        parts += [x1 * cos - x2 * sin, x2 * cos + x1 * sin, y[r0 + ROT_DIM:r0 + HEAD_DIM]]
    out = jnp.concatenate(parts, axis=0)
    return out if scale is None else out * scale


def _proj_kernel(x_ref, g_ref, wt_ref, cos_ref, sin_ref, *out_refs, with_indexer):
    hn = _rms(x_ref[0], g_ref[...]).astype(MXU_DTYPE)
    tm = hn.shape[0]
    cos = cos_ref[0]
    sin = sin_ref[0]

    def proj(r0, rows):
        return lax.dot_general(wt_ref[r0:r0 + rows, :], hn, _NT,
                               preferred_element_type=jnp.float32)

    qt_ref, k_ref, vt_ref = out_refs[:3]
    q = proj(0, QD)
    qt_ref[0] = _rope_heads_t(q, N_Q_HEADS, cos, sin, HEAD_DIM ** -0.5 * LOG2E).astype(MXU_DTYPE)
    k = _rope_heads_t(proj(QD, KVD), N_KV_HEADS, cos, sin)
    k_ref[0] = k.T.astype(MXU_DTYPE)
    v = proj(QD + KVD, KVD)
    ones = jnp.ones((PACKED_ROWS, tm), jnp.float32)
    v_rows = []
    for h in range(N_KV_HEADS):
        v_rows += [v[h * HEAD_DIM:(h + 1) * HEAD_DIM], ones]
    vt_ref[0, 0] = jnp.concatenate(v_rows, axis=0).astype(MXU_DTYPE)
    if with_indexer:
        qit_ref, ki_ref, wit_ref = out_refs[3:]
        r0 = QD + 2 * KVD
        qi = _rope_heads_t(proj(r0, IDX_HEADS * IDX_DIM), IDX_HEADS, cos, sin)
        qit_ref[0] = qi.astype(MXU_DTYPE)
        r0 += IDX_HEADS * IDX_DIM
        ki = proj(r0, 2 * IDX_DIM)
        ki = jnp.concatenate([_rope_heads_t(ki[:IDX_DIM], 1, cos, sin), ki[IDX_DIM:]], axis=0)
        ki_ref[0] = ki.T.astype(MXU_DTYPE)
        r0 += 2 * IDX_DIM
        wi = proj(r0, 2 * SUBLANES)
        wit_ref[0] = wi[:IDX_HEADS] * ((IDX_HEADS ** -0.5) * (IDX_DIM ** -0.5))


def _project(x, g, wt, cos_t, sin_t, with_indexer):
    B, S, D = x.shape
    tm = TOK_TILE
    n_rows = wt.shape[0]
    bf = MXU_DTYPE
    vr = N_KV_HEADS * V_ROWS
    out_shape = [jax.ShapeDtypeStruct((B, QD, S), bf),
                 jax.ShapeDtypeStruct((B, S, KVD), bf),
                 jax.ShapeDtypeStruct((B, S // tm, vr, tm), bf)]
    out_specs = [pl.BlockSpec((1, QD, tm), lambda b, i: (b, 0, i)),
                 pl.BlockSpec((1, tm, KVD), lambda b, i: (b, i, 0)),
                 pl.BlockSpec((1, 1, vr, tm), lambda b, i: (b, i, 0, 0))]
    if with_indexer:
        out_shape += [jax.ShapeDtypeStruct((B, IDX_HEADS * IDX_DIM, S), bf),
                      jax.ShapeDtypeStruct((B, S, 2 * IDX_DIM), bf),
                      jax.ShapeDtypeStruct((B, IDX_HEADS, S), jnp.float32)]
        out_specs += [pl.BlockSpec((1, IDX_HEADS * IDX_DIM, tm), lambda b, i: (b, 0, i)),
                      pl.BlockSpec((1, tm, 2 * IDX_DIM), lambda b, i: (b, i, 0)),
                      pl.BlockSpec((1, IDX_HEADS, tm), lambda b, i: (b, 0, i))]
    return pl.pallas_call(
        functools.partial(_proj_kernel, with_indexer=with_indexer),
        grid=(B, S // tm),
        in_specs=[pl.BlockSpec((1, tm, D), lambda b, i: (b, i, 0)),
                  pl.BlockSpec((1, D), lambda b, i: (0, 0)),
                  pl.BlockSpec((n_rows, D), lambda b, i: (0, 0)),
                  pl.BlockSpec((1, ROT_HALF, tm), lambda b, i: (b, 0, i)),
                  pl.BlockSpec((1, ROT_HALF, tm), lambda b, i: (b, 0, i))],
        out_specs=out_specs,
        out_shape=out_shape,
        compiler_params=pltpu.CompilerParams(
            dimension_semantics=("arbitrary", "arbitrary"), vmem_limit_bytes=VMEM_LIMIT),
        name="proj_indexer" if with_indexer else "proj",
    )(x, g.reshape(1, D), wt, cos_t, sin_t)


def _head_weights(qt, h):
    blocks = [qt[(GROUP * h + g) * HEAD_DIM:(GROUP * h + g + 1) * HEAD_DIM, :] for g in range(GROUP)]
    w = jnp.concatenate(blocks, axis=1)
    z = jnp.zeros_like(w)
    return jnp.concatenate([w, z] if h % 2 == 0 else [z, w], axis=0)


def _bias_identity():
    r = lax.broadcasted_iota(jnp.int32, (LANES, GROUP * BLOCK), 0)
    c = lax.broadcasted_iota(jnp.int32, (LANES, GROUP * BLOCK), 1)
    return jnp.where(r == (c & (BLOCK - 1)), 1.0, 0.0).astype(MXU_DTYPE)


def _col_reduce(x, op):
    rows, n = x.shape
    part = op(x.reshape(rows // SUBLANES, SUBLANES, n), axis=0)
    return op(part, axis=0, keepdims=True)


def _swa_kernel(qt_ref, kp_ref, kc_ref, vp_ref, vc_ref, sink_ref, o_ref, s_ref):
    n = pl.program_id(1)
    qt = qt_ref[0]
    c = lax.broadcasted_iota(jnp.int32, (2 * BLOCK, BLOCK), 0)
    i = lax.broadcasted_iota(jnp.int32, (2 * BLOCK, BLOCK), 1)
    first_slot = jnp.where(n > 0, 0, BLOCK)
    valid = (c > i) & (c <= i + BLOCK) & (c >= first_slot)
    bias = jnp.where(valid, 0.0, MASK_NEG).astype(MXU_DTYPE)
    ident = _bias_identity()

    def scores(h):
        pair = slice(LANES * (h // 2), LANES * (h // 2 + 1))
        kcat = jnp.concatenate([kp_ref[0, :, pair], kc_ref[0, :, pair]], axis=0)
        w = jnp.concatenate([_head_weights(qt, h), ident], axis=0)
        s_ref[h % 2] = jnp.dot(jnp.concatenate([kcat, bias], axis=1), w,
                               preferred_element_type=jnp.float32)

    def finish(h):
        s = s_ref[h % 2]
        sink = sink_ref[:, GROUP * h * BLOCK:GROUP * (h + 1) * BLOCK]
        m = jnp.maximum(_col_reduce(s, jnp.max), sink)
        p = jnp.exp2(s - m).astype(MXU_DTYPE)
        rows = slice(V_ROWS * h, V_ROWS * (h + 1))
        vcat = jnp.concatenate([vp_ref[0, 0, rows, :], vc_ref[0, 0, rows, :]], axis=1)
        o = jnp.dot(vcat, p, preferred_element_type=jnp.float32)
        o = o[:HEAD_DIM] / (o[HEAD_DIM:HEAD_DIM + 1] + jnp.exp2(sink - m))
        return [o[:, g * BLOCK:(g + 1) * BLOCK] for g in range(GROUP)]

    outs = []
    scores(0)
    for h in range(N_KV_HEADS):
        if h + 1 < N_KV_HEADS:
            scores(h + 1)
        outs += finish(h)
    o_ref[0] = jnp.concatenate(outs, axis=0).T.astype(MXU_DTYPE)


def _swa_attention(qt, k, vt, sinks):
    B, _, S = qt.shape
    nb = S // BLOCK
    per_tile = TOK_TILE // BLOCK
    vr = N_KV_HEADS * V_ROWS
    sink_row = jnp.repeat(sinks.astype(jnp.float32) * LOG2E, BLOCK).reshape(1, N_Q_HEADS * BLOCK)

    def prev(n):
        return jnp.maximum(n - 1, 0)

    return pl.pallas_call(
        _swa_kernel,
        grid=(B, nb),
        in_specs=[pl.BlockSpec((1, QD, BLOCK), lambda b, n: (b, 0, n)),
                  pl.BlockSpec((1, BLOCK, KVD), lambda b, n: (b, prev(n), 0)),
                  pl.BlockSpec((1, BLOCK, KVD), lambda b, n: (b, n, 0)),
                  pl.BlockSpec((1, 1, vr, BLOCK), lambda b, n: (b, prev(n) // per_tile, 0, prev(n) % per_tile)),
                  pl.BlockSpec((1, 1, vr, BLOCK), lambda b, n: (b, n // per_tile, 0, n % per_tile)),
                  pl.BlockSpec((1, N_Q_HEADS * BLOCK), lambda b, n: (0, 0))],
        out_specs=pl.BlockSpec((1, BLOCK, QD), lambda b, n: (b, n, 0)),
        out_shape=jax.ShapeDtypeStruct((B, S, QD), MXU_DTYPE),
        scratch_shapes=[pltpu.VMEM((2, 2 * BLOCK, GROUP * BLOCK), jnp.float32)],
        compiler_params=pltpu.CompilerParams(
            dimension_semantics=("arbitrary", "arbitrary"), vmem_limit_bytes=VMEM_LIMIT),
        name="swa_attention",
    )(qt, k, k, vt, vt, sink_row)


def _bit_planes(words):
    a = list(words)
    j, m = 16, 0x0000FFFF
    while j:
        k = 0
        while k < 32:
            t = (a[k] ^ lax.shift_right_logical(a[k + j], jnp.int32(j))) & m
            a[k] = a[k] ^ t
            a[k + j] = a[k + j] ^ (t << j)
            k = (k + j + 1) & ~j
        j >>= 1
        m ^= (m << j) & 0xFFFFFFFF
    return a


def _popcount_cols(x):
    return _col_reduce(lax.population_count(x), jnp.sum)


def _dsa_kernel(qt_ref, qit_ref, wit_ref, k_ref, ki_ref, vt_ref, o_ref,
                skey_ref, planes_ref, wq_ref, m_ref, acc_ref, tie_ref,
                s_ref, p_ref, cmax_ref, bias_ref, lg_ref, score_ref, *, n_sel, n_blocks):
    t = pl.program_id(1)
    kc = TOK_TILE
    per_chunk = kc // BLOCK
    n_idx = jnp.minimum(t, n_blocks - 1)
    idx_chunks = n_idx // per_chunk + 1
    att_chunks = (t + per_chunk - 1) // per_chunk
    n_keys = skey_ref.shape[0]
    idx_bits = (n_keys - 1).bit_length()
    t_row = n_idx * BLOCK + lax.broadcasted_iota(jnp.int32, (1, BLOCK), 1)

    def chunk_rows(j):
        return pl.ds(pl.multiple_of(j * kc, kc), kc)

    def key_index(j):
        return j * kc + lax.broadcasted_iota(jnp.int32, (kc, BLOCK), 0)

    @pl.when((pl.program_id(0) == 0) & (t == 0))
    def _():
        planes_ref[...] = jnp.zeros(planes_ref.shape, jnp.int32)

    qt = qt_ref[0]
    ident = _bias_identity()
    for h in range(N_KV_HEADS):
        wq_ref[h] = jnp.concatenate([_head_weights(qt, h), ident], axis=0)
    qit = qit_ref[0]
    wqi = jnp.concatenate([qit[h * IDX_DIM:(h + 1) * IDX_DIM, :] for h in range(IDX_HEADS)], axis=1)
    wqi = jnp.concatenate([wqi, jnp.zeros_like(wqi)], axis=0)
    wit = wit_ref[0]
    m_ref[...] = jnp.full(m_ref.shape, NEG_INIT, jnp.float32)
    acc_ref[...] = jnp.zeros(acc_ref.shape, jnp.float32)

    half = IDX_HEADS // 2

    def index_logits(j, slot):
        cols = slice(slot * half * BLOCK, (slot + 1) * half * BLOCK)
        lg_ref[slot] = jnp.dot(ki_ref[0, chunk_rows(j), :], wqi[:, cols],
                               preferred_element_type=jnp.float32)

    def weighted_relu(slot):
        lg = lg_ref[slot]
        out = None
        for h in range(half):
            term = jnp.maximum(lg[:, h * BLOCK:(h + 1) * BLOCK], 0.0) * wit[slot * half + h:slot * half + h + 1, :]
            out = term if out is None else out + term
        return out

    group = 32 * SUBLANES

    def index_first_half(j):
        index_logits(j, 1)
        score_ref[...] = weighted_relu(0)

    def index_keys(j):
        index_logits(jnp.minimum(j + 1, idx_chunks - 1), 0)
        score = score_ref[...] + weighted_relu(1)
        bits = pltpu.bitcast(score, jnp.int32)
        skey = bits ^ ((bits >> 31) & 0x7FFFFFFF)
        skey = jnp.where(bits == INT_MIN, 0, skey)
        skey_ref[chunk_rows(j), :] = jnp.where(key_index(j) <= t_row, skey, INT_MIN)

    def index_planes(j, g):
        base = pl.multiple_of(j * kc + g * group, group)
        words = [skey_ref[pl.ds(base + SUBLANES * i, SUBLANES), :] ^ INT_MIN for i in range(32)]
        row0 = pl.multiple_of((j * (kc // group) + g) * SUBLANES, SUBLANES)
        for p, plane in enumerate(_bit_planes(words)):
            planes_ref[p, pl.ds(row0, SUBLANES), :] = plane

    index_stages = [index_first_half, index_keys] + [
        functools.partial(index_planes, g=g) for g in range(kc // group)]

    def scores(j, h, slot):
        pair = slice(LANES * (h // 2), LANES * (h // 2 + 1))
        lhs = jnp.concatenate([k_ref[0, chunk_rows(j), pair], bias_ref[chunk_rows(j), :]], axis=1)
        s = jnp.dot(lhs, wq_ref[h], preferred_element_type=jnp.float32)
        s_ref[slot] = s
        cmax_ref[slot] = _col_reduce(s, jnp.max)

    def accumulate(j, h, slot):
        m_old = m_ref[h]
        m_new = jnp.maximum(m_old, cmax_ref[slot])
        for r0 in range(0, kc, EXP_ROWS):
            p_ref[slot, r0:r0 + EXP_ROWS, :] = jnp.exp2(
                s_ref[slot, r0:r0 + EXP_ROWS, :] - m_new).astype(MXU_DTYPE)
        vt = vt_ref[0, j, V_ROWS * h:V_ROWS * (h + 1), :]
        acc_ref[h] = jnp.exp2(m_old - m_new) * acc_ref[h] + jnp.dot(
            vt, p_ref[slot], preferred_element_type=jnp.float32)
        m_ref[h] = m_new

    index_logits(0, 0)

    @pl.when(t > 0)
    def _():
        scores(0, 0, 0)

    def both(j, carry):
        for h in range(N_KV_HEADS):
            if h + 1 < N_KV_HEADS:
                scores(j, h + 1, (h + 1) % 2)
            else:
                scores(jnp.minimum(j + 1, att_chunks - 1), 0, 0)
            index_stages[h](j)
            accumulate(j, h, h % 2)
        return carry

    lax.fori_loop(0, att_chunks, both, 0)

    @pl.when(idx_chunks > att_chunks)
    def _():
        for stage in index_stages:
            stage(att_chunks)

    @pl.when(t > 0)
    def _():
        outs = []
        for h in range(N_KV_HEADS):
            acc = acc_ref[h]
            o = acc[:HEAD_DIM] / acc[HEAD_DIM:HEAD_DIM + 1]
            outs += [o[:, g * BLOCK:(g + 1) * BLOCK] for g in range(GROUP)]
        o_ref[0] = jnp.concatenate(outs, axis=0).T.astype(MXU_DTYPE)

    plane_rows = lax.broadcasted_iota(jnp.int32, (n_keys // 32, BLOCK), 0)
    active = jnp.where(plane_rows < idx_chunks * (kc // 32), -1, 0)
    zero_row = jnp.zeros((1, BLOCK), jnp.int32)
    cnt_above, thr_u = zero_row, zero_row
    for p in range(32):
        ones = active & planes_ref[p]
        c1 = _popcount_cols(ones)
        take = (cnt_above + c1) >= n_sel
        active = jnp.where(take, ones, active ^ ones)
        cnt_above = jnp.where(take, cnt_above, cnt_above + c1)
        bit = (1 << (31 - p)) - (1 << 32 if p == 0 else 0)
        thr_u = jnp.where(take, thr_u | bit, thr_u)
    cnt_ge = cnt_above + _popcount_cols(active)
    thr = jnp.maximum(thr_u ^ INT_MIN, INT_MIN + 1)
    excess = (cnt_ge > n_sel) & (thr_u != 0)
    tie_ref[...] = jnp.full((1, BLOCK), 2 ** 30, jnp.int32)

    @pl.when(jnp.max(jnp.where(excess, 1, 0)) > 0)
    def _():
        def count(pred):
            def body(j, acc):
                hit = jnp.where(pred(skey_ref[chunk_rows(j), :], j), 1, 0)
                return acc + jnp.sum(hit.reshape(kc // SUBLANES, SUBLANES, BLOCK), axis=0)
            acc = lax.fori_loop(0, idx_chunks, body, jnp.zeros((SUBLANES, BLOCK), jnp.int32))
            return jnp.sum(acc, axis=0, keepdims=True)

        need = n_sel - count(lambda sk, j: sk > thr)

        def idx_step(b, x):
            cand = x | lax.shift_left(jnp.int32(1), idx_bits - 1 - b)
            below = count(lambda sk, j: (sk == thr) & (key_index(j) < cand))
            return jnp.where(below < need, cand, x)

        x = lax.fori_loop(0, idx_bits, idx_step, zero_row)
        tie_ref[...] = jnp.where(excess, x, 2 ** 30)

    tie_x = tie_ref[...]

    def select_bias(j, carry):
        sk = skey_ref[chunk_rows(j), :]
        sel = (sk > thr) | ((sk == thr) & (key_index(j) <= tie_x))
        bias_ref[chunk_rows(j), :] = jnp.where(sel, 0.0, MASK_NEG).astype(MXU_DTYPE)
        return carry

    lax.fori_loop(0, idx_chunks, select_bias, 0)


def _dsa_attention(qt, qit, wit, k, ki, vt):
    B, _, S = qt.shape
    nb = S // BLOCK
    n_sel = min(TOPK_MAX, S // 4)

    def attended(t):
        return jnp.maximum(t - 1, 0)

    def indexed(t):
        return jnp.minimum(t, nb - 1)

    return pl.pallas_call(
        functools.partial(_dsa_kernel, n_sel=n_sel, n_blocks=nb),
        grid=(B, nb + 1),
        in_specs=[pl.BlockSpec((1, QD, BLOCK), lambda b, t: (b, 0, attended(t))),
                  pl.BlockSpec((1, IDX_HEADS * IDX_DIM, BLOCK), lambda b, t: (b, 0, indexed(t))),
                  pl.BlockSpec((1, IDX_HEADS, BLOCK), lambda b, t: (b, 0, indexed(t))),
                  pl.BlockSpec((1, S, KVD), lambda b, t: (b, 0, 0)),
                  pl.BlockSpec((1, S, 2 * IDX_DIM), lambda b, t: (b, 0, 0)),
                  pl.BlockSpec((1, S // TOK_TILE, N_KV_HEADS * V_ROWS, TOK_TILE), lambda b, t: (b, 0, 0, 0))],
        out_specs=pl.BlockSpec((1, BLOCK, QD), lambda b, t: (b, attended(t), 0)),
        out_shape=jax.ShapeDtypeStruct((B, S, QD), MXU_DTYPE),
        scratch_shapes=[pltpu.VMEM((S, BLOCK), jnp.int32),
                        pltpu.VMEM((32, S // 32, BLOCK), jnp.int32),
                        pltpu.VMEM((N_KV_HEADS, 2 * LANES, GROUP * BLOCK), MXU_DTYPE),
                        pltpu.VMEM((N_KV_HEADS, 1, GROUP * BLOCK), jnp.float32),
                        pltpu.VMEM((N_KV_HEADS, V_ROWS, GROUP * BLOCK), jnp.float32),
                        pltpu.VMEM((1, BLOCK), jnp.int32),
                        pltpu.VMEM((2, TOK_TILE, GROUP * BLOCK), jnp.float32),
                        pltpu.VMEM((2, TOK_TILE, GROUP * BLOCK), MXU_DTYPE),
                        pltpu.VMEM((2, 1, GROUP * BLOCK), jnp.float32),
                        pltpu.VMEM((S, BLOCK), MXU_DTYPE),
                        pltpu.VMEM((2, TOK_TILE, IDX_HEADS // 2 * BLOCK), jnp.float32),
                        pltpu.VMEM((TOK_TILE, BLOCK), jnp.float32)],
        compiler_params=pltpu.CompilerParams(
            dimension_semantics=("arbitrary", "arbitrary"), vmem_limit_bytes=VMEM_LIMIT),
        name="dsa_attention",
    )(qt, qit, wit, k, ki, vt)


def _post_kernel(a_ref, x_ref, g_ref, wo_ref, wu_ref, wd_ref, o_ref):
    mix = jnp.dot(a_ref[...], wo_ref[...], preferred_element_type=jnp.float32)
    x1 = x_ref[...] + _rms(mix, g_ref[1:2, :])
    hn = _rms(x1, g_ref[2:3, :]).astype(MXU_DTYPE)
    acc = jnp.zeros(x1.shape, jnp.float32)
    for c in range(D_FF // FF_CHUNK):
        cols = slice(c * FF_CHUNK, (c + 1) * FF_CHUNK)
        a = jnp.maximum(jnp.dot(hn, wu_ref[:, cols], preferred_element_type=jnp.float32), 0.0)
        acc = acc + jnp.dot((a * a).astype(MXU_DTYPE), wd_ref[cols, :],
                            preferred_element_type=jnp.float32)
    o_ref[...] = x1 + _rms(acc, g_ref[3:4, :])


def _post_attention(attn, x, g, w_out, w_up, w_down):
    T, D = x.shape
    tm = ROW_TILE
    return pl.pallas_call(
        _post_kernel,
        grid=(T // tm,),
        in_specs=[pl.BlockSpec((tm, QD), lambda i: (i, 0)),
                  pl.BlockSpec((tm, D), lambda i: (i, 0)),
                  pl.BlockSpec((4, D), lambda i: (0, 0)),
                  pl.BlockSpec((QD, D), lambda i: (0, 0), pipeline_mode=pl.Buffered(1)),
                  pl.BlockSpec((D, D_FF), lambda i: (0, 0), pipeline_mode=pl.Buffered(1)),
                  pl.BlockSpec((D_FF, D), lambda i: (0, 0), pipeline_mode=pl.Buffered(1))],
        out_specs=pl.BlockSpec((tm, D), lambda i: (i, 0)),
        out_shape=jax.ShapeDtypeStruct((T, D), jnp.float32),
        compiler_params=pltpu.CompilerParams(
            dimension_semantics=("arbitrary",), vmem_limit_bytes=VMEM_LIMIT),
        name="post_attention",
    )(attn, x, g, w_out, w_up, w_down)


def _pad_rows(w, rows):
    return jnp.concatenate([w, jnp.zeros((rows - w.shape[0], w.shape[1]), w.dtype)], axis=0)


def kernel(x, positions, norm_gains, w_mlp_up, w_mlp_down, a_w_in, a_w_out, a_sinks, b_w_in, b_w_out):
    B, S, D = x.shape
    bf = MXU_DTYPE
    cos_t, sin_t = _rope_tables(positions)

    wt_a = a_w_in[0].T.astype(bf)
    qt, k, vt = _project(x, norm_gains[0, 0], wt_a, cos_t, sin_t, with_indexer=False)
    attn = _swa_attention(qt, k, vt, a_sinks[0])
    x = _post_attention(attn.reshape(B * S, QD), x.reshape(B * S, D), norm_gains[0],
                        a_w_out[0].astype(bf), w_mlp_up[0].astype(bf), w_mlp_down[0].astype(bf))
    x = x.reshape(B, S, D)

    wb = b_w_in[0].T
    r_qi = QD + 2 * KVD
    r_ki = r_qi + IDX_HEADS * IDX_DIM
    r_wi = r_ki + IDX_DIM
    wt_b = jnp.concatenate([wb[:r_ki], _pad_rows(wb[r_ki:r_wi], 2 * IDX_DIM),
                            _pad_rows(wb[r_wi:], 2 * SUBLANES)], axis=0).astype(bf)
    qt, k, vt, qit, ki, wit = _project(x, norm_gains[1, 0], wt_b, cos_t, sin_t, with_indexer=True)
    attn = _dsa_attention(qt, qit, wit, k, ki, vt)
    x = _post_attention(attn.reshape(B * S, QD), x.reshape(B * S, D), norm_gains[1],
                        b_w_out[0].astype(bf), w_mlp_up[1].astype(bf), w_mlp_down[1].astype(bf))
    return x.reshape(B, S, D)
```

```python
import functools
import math

import jax
import jax.numpy as jnp
from jax import lax
from jax.experimental import pallas as pl
from jax.experimental.pallas import tpu as pltpu

D_MODEL = 1024
HEAD_DIM = 64
N_Q_HEADS = 16
N_KV_HEADS = 4
GROUP = 4
ROT_DIM = 16
ROT_HALF = ROT_DIM // 2
ROPE_THETA = 500000.0
BLOCK = 128
IDX_HEADS = 8
IDX_DIM = 64
TOPK_MAX = 256
D_FF = 4 * D_MODEL
EPS = 1e-6
QD = N_Q_HEADS * HEAD_DIM
KVD = N_KV_HEADS * HEAD_DIM

LANES = 128
SUBLANES = 8
PACKED_ROWS = 16
TOK_TILE = 512
ROW_TILE = 512
FF_CHUNK = 1024
EXP_ROWS = 64
SELECT_CLASSES = 4
VMEM_LIMIT = 56 * 1024 * 1024
V_ROWS = HEAD_DIM + PACKED_ROWS
INT_MIN = -2 ** 31
NEG_INIT = -1e30
MASK_NEG = -1e9
LOG2E = math.log2(math.e)
MXU_DTYPE = jnp.bfloat16

_NT = (((1,), (1,)), ((), ()))


def _rms(x, g):
    return x * lax.rsqrt(jnp.mean(x * x, axis=-1, keepdims=True) + EPS) * g


def _rope_table_kernel(pos_ref, inv_ref, cos_ref, sin_ref):
    ang = pos_ref[0].astype(jnp.float32) * inv_ref[:, :1]
    cos_ref[0] = jnp.cos(ang)
    sin_ref[0] = jnp.sin(ang)


def _rope_tables(positions):
    B, S = positions.shape
    inv = ROPE_THETA ** (-jnp.arange(0, ROT_DIM, 2, dtype=jnp.float32) / ROT_DIM)
    inv = jnp.broadcast_to(inv[:, None], (ROT_HALF, LANES))
    out = jax.ShapeDtypeStruct((B, ROT_HALF, S), jnp.float32)
    return pl.pallas_call(
        _rope_table_kernel,
        grid=(B,),
        in_specs=[pl.BlockSpec((1, 1, S), lambda b: (b, 0, 0)),
                  pl.BlockSpec((ROT_HALF, LANES), lambda b: (0, 0))],
        out_specs=[pl.BlockSpec((1, ROT_HALF, S), lambda b: (b, 0, 0))] * 2,
        out_shape=[out, out],
        name="rope_tables",
    )(positions.reshape(B, 1, S), inv)


def _rope_heads_t(y, n_heads, cos, sin, scale=None):
    parts = []
    for h in range(n_heads):
        r0 = h * HEAD_DIM
        x1 = y[r0:r0 + ROT_HALF]
        x2 = y[r0 + ROT_HALF:r0 + ROT_DIM]
        parts += [x1 * cos - x2 * sin, x2 * cos + x1 * sin, y[r0 + ROT_DIM:r0 + HEAD_DIM]]
    out = jnp.concatenate(parts, axis=0)
    return out if scale is None else out * scale


def _proj_kernel(x_ref, g_ref, wt_ref, cos_ref, sin_ref, *out_refs, with_indexer):
    hn = _rms(x_ref[0], g_ref[...]).astype(MXU_DTYPE)
    tm = hn.shape[0]
    cos = cos_ref[0]
    sin = sin_ref[0]

    def proj(r0, rows):
        return lax.dot_general(wt_ref[r0:r0 + rows, :], hn, _NT,
                               preferred_element_type=jnp.float32)

    qt_ref, k_ref, vt_ref = out_refs[:3]
    q = proj(0, QD)
    qt_ref[0] = _rope_heads_t(q, N_Q_HEADS, cos, sin, HEAD_DIM ** -0.5 * LOG2E).astype(MXU_DTYPE)
    k = _rope_heads_t(proj(QD, KVD), N_KV_HEADS, cos, sin)
    k_ref[0] = k.T.astype(MXU_DTYPE)
    v = proj(QD + KVD, KVD)
    ones = jnp.ones((PACKED_ROWS, tm), jnp.float32)
    v_rows = []
    for h in range(N_KV_HEADS):
        v_rows += [v[h * HEAD_DIM:(h + 1) * HEAD_DIM], ones]
    vt_ref[0, 0] = jnp.concatenate(v_rows, axis=0).astype(MXU_DTYPE)
    if with_indexer:
        qit_ref, ki_ref, wit_ref = out_refs[3:]
        r0 = QD + 2 * KVD
        qi = _rope_heads_t(proj(r0, IDX_HEADS * IDX_DIM), IDX_HEADS, cos, sin)
        qit_ref[0] = qi.astype(MXU_DTYPE)
        r0 += IDX_HEADS * IDX_DIM
        ki = proj(r0, 2 * IDX_DIM)
        ki = jnp.concatenate([_rope_heads_t(ki[:IDX_DIM], 1, cos, sin), ki[IDX_DIM:]], axis=0)
        ki_ref[0] = ki.T.astype(MXU_DTYPE)
        r0 += 2 * IDX_DIM
        wi = proj(r0, 2 * SUBLANES)
        wit_ref[0] = wi[:IDX_HEADS] * ((IDX_HEADS ** -0.5) * (IDX_DIM ** -0.5))


def _project(x, g, wt, cos_t, sin_t, with_indexer):
    B, S, D = x.shape
    tm = TOK_TILE
    n_rows = wt.shape[0]
    bf = MXU_DTYPE
    vr = N_KV_HEADS * V_ROWS
    out_shape = [jax.ShapeDtypeStruct((B, QD, S), bf),
                 jax.ShapeDtypeStruct((B, S, KVD), bf),
                 jax.ShapeDtypeStruct((B, S // tm, vr, tm), bf)]
    out_specs = [pl.BlockSpec((1, QD, tm), lambda b, i: (b, 0, i)),
                 pl.BlockSpec((1, tm, KVD), lambda b, i: (b, i, 0)),
                 pl.BlockSpec((1, 1, vr, tm), lambda b, i: (b, i, 0, 0))]
    if with_indexer:
        out_shape += [jax.ShapeDtypeStruct((B, IDX_HEADS * IDX_DIM, S), bf),
                      jax.ShapeDtypeStruct((B, S, 2 * IDX_DIM), bf),
                      jax.ShapeDtypeStruct((B, IDX_HEADS, S), jnp.float32)]
        out_specs += [pl.BlockSpec((1, IDX_HEADS * IDX_DIM, tm), lambda b, i: (b, 0, i)),
                      pl.BlockSpec((1, tm, 2 * IDX_DIM), lambda b, i: (b, i, 0)),
                      pl.BlockSpec((1, IDX_HEADS, tm), lambda b, i: (b, 0, i))]
    return pl.pallas_call(
        functools.partial(_proj_kernel, with_indexer=with_indexer),
        grid=(B, S // tm),
        in_specs=[pl.BlockSpec((1, tm, D), lambda b, i: (b, i, 0)),
                  pl.BlockSpec((1, D), lambda b, i: (0, 0)),
                  pl.BlockSpec((n_rows, D), lambda b, i: (0, 0)),
                  pl.BlockSpec((1, ROT_HALF, tm), lambda b, i: (b, 0, i)),
                  pl.BlockSpec((1, ROT_HALF, tm), lambda b, i: (b, 0, i))],
        out_specs=out_specs,
        out_shape=out_shape,
        compiler_params=pltpu.CompilerParams(
            dimension_semantics=("arbitrary", "arbitrary"), vmem_limit_bytes=VMEM_LIMIT),
        name="proj_indexer" if with_indexer else "proj",
    )(x, g.reshape(1, D), wt, cos_t, sin_t)


def _head_weights(qt, h):
    blocks = [qt[(GROUP * h + g) * HEAD_DIM:(GROUP * h + g + 1) * HEAD_DIM, :] for g in range(GROUP)]
    w = jnp.concatenate(blocks, axis=1)
    z = jnp.zeros_like(w)
    return jnp.concatenate([w, z] if h % 2 == 0 else [z, w], axis=0)


def _bias_identity():
    r = lax.broadcasted_iota(jnp.int32, (LANES, GROUP * BLOCK), 0)
    c = lax.broadcasted_iota(jnp.int32, (LANES, GROUP * BLOCK), 1)
    return jnp.where(r == (c & (BLOCK - 1)), 1.0, 0.0).astype(MXU_DTYPE)


def _col_reduce(x, op):
    rows, n = x.shape
    part = op(x.reshape(rows // SUBLANES, SUBLANES, n), axis=0)
    return op(part, axis=0, keepdims=True)


def _swa_kernel(qt_ref, kp_ref, kc_ref, vp_ref, vc_ref, sink_ref, o_ref, s_ref):
    n = pl.program_id(1)
    qt = qt_ref[0]
    c = lax.broadcasted_iota(jnp.int32, (2 * BLOCK, BLOCK), 0)
    i = lax.broadcasted_iota(jnp.int32, (2 * BLOCK, BLOCK), 1)
    first_slot = jnp.where(n > 0, 0, BLOCK)
    valid = (c > i) & (c <= i + BLOCK) & (c >= first_slot)
    bias = jnp.where(valid, 0.0, MASK_NEG).astype(MXU_DTYPE)
    ident = _bias_identity()

    def scores(h):
        pair = slice(LANES * (h // 2), LANES * (h // 2 + 1))
        kcat = jnp.concatenate([kp_ref[0, :, pair], kc_ref[0, :, pair]], axis=0)
        w = jnp.concatenate([_head_weights(qt, h), ident], axis=0)
        s_ref[h % 2] = jnp.dot(jnp.concatenate([kcat, bias], axis=1), w,
                               preferred_element_type=jnp.float32)

    def finish(h):
        s = s_ref[h % 2]
        sink = sink_ref[:, GROUP * h * BLOCK:GROUP * (h + 1) * BLOCK]
        m = jnp.maximum(_col_reduce(s, jnp.max), sink)
        p = jnp.exp2(s - m).astype(MXU_DTYPE)
        rows = slice(V_ROWS * h, V_ROWS * (h + 1))
        vcat = jnp.concatenate([vp_ref[0, 0, rows, :], vc_ref[0, 0, rows, :]], axis=1)
        o = jnp.dot(vcat, p, preferred_element_type=jnp.float32)
        o = o[:HEAD_DIM] / (o[HEAD_DIM:HEAD_DIM + 1] + jnp.exp2(sink - m))
        return [o[:, g * BLOCK:(g + 1) * BLOCK] for g in range(GROUP)]

    outs = []
    scores(0)
    for h in range(N_KV_HEADS):
        if h + 1 < N_KV_HEADS:
            scores(h + 1)
        outs += finish(h)
    o_ref[0] = jnp.concatenate(outs, axis=0).T.astype(MXU_DTYPE)


def _swa_attention(qt, k, vt, sinks):
    B, _, S = qt.shape
    nb = S // BLOCK
    per_tile = TOK_TILE // BLOCK
    vr = N_KV_HEADS * V_ROWS
    sink_row = jnp.repeat(sinks.astype(jnp.float32) * LOG2E, BLOCK).reshape(1, N_Q_HEADS * BLOCK)

    def prev(n):
        return jnp.maximum(n - 1, 0)

    return pl.pallas_call(
        _swa_kernel,
        grid=(B, nb),
        in_specs=[pl.BlockSpec((1, QD, BLOCK), lambda b, n: (b, 0, n)),
                  pl.BlockSpec((1, BLOCK, KVD), lambda b, n: (b, prev(n), 0)),
                  pl.BlockSpec((1, BLOCK, KVD), lambda b, n: (b, n, 0)),
                  pl.BlockSpec((1, 1, vr, BLOCK), lambda b, n: (b, prev(n) // per_tile, 0, prev(n) % per_tile)),
                  pl.BlockSpec((1, 1, vr, BLOCK), lambda b, n: (b, n // per_tile, 0, n % per_tile)),
                  pl.BlockSpec((1, N_Q_HEADS * BLOCK), lambda b, n: (0, 0))],
        out_specs=pl.BlockSpec((1, BLOCK, QD), lambda b, n: (b, n, 0)),
        out_shape=jax.ShapeDtypeStruct((B, S, QD), MXU_DTYPE),
        scratch_shapes=[pltpu.VMEM((2, 2 * BLOCK, GROUP * BLOCK), jnp.float32)],
        compiler_params=pltpu.CompilerParams(
            dimension_semantics=("arbitrary", "arbitrary"), vmem_limit_bytes=VMEM_LIMIT),
        name="swa_attention",
    )(qt, k, k, vt, vt, sink_row)


def _bit_planes(words):
    a = list(words)
    j, m = 16, 0x0000FFFF
    while j:
        k = 0
        while k < 32:
            t = (a[k] ^ lax.shift_right_logical(a[k + j], jnp.int32(j))) & m
            a[k] = a[k] ^ t
            a[k + j] = a[k + j] ^ (t << j)
            k = (k + j + 1) & ~j
        j >>= 1
        m ^= (m << j) & 0xFFFFFFFF
    return a


def _popcount_cols(x):
    return _col_reduce(lax.population_count(x), jnp.sum)


def _dsa_kernel(qt_ref, qit_ref, wit_ref, k_ref, ki_ref, vt_ref, o_ref,
                skey_ref, planes_ref, wq_ref, m_ref, acc_ref, sel_ref,
                s_ref, p_ref, cmax_ref, bias_ref, lg_ref, score_ref, *, n_sel, n_blocks):
    t = pl.program_id(1)
    kc = TOK_TILE
    per_chunk = kc // BLOCK
    n_idx = jnp.minimum(t, n_blocks - 1)
    idx_chunks = n_idx // per_chunk + 1
    att_chunks = (t + per_chunk - 1) // per_chunk
    n_keys = skey_ref.shape[0]
    idx_bits = (n_keys - 1).bit_length()
    t_row = n_idx * BLOCK + lax.broadcasted_iota(jnp.int32, (1, BLOCK), 1)

    def chunk_rows(j):
        return pl.ds(pl.multiple_of(j * kc, kc), kc)

    def key_index(j):
        return j * kc + lax.broadcasted_iota(jnp.int32, (kc, BLOCK), 0)

    @pl.when((pl.program_id(0) == 0) & (t == 0))
    def _():
        planes_ref[...] = jnp.zeros(planes_ref.shape, jnp.int32)

    qt = qt_ref[0]
    ident = _bias_identity()
    for h in range(N_KV_HEADS):
        wq_ref[h] = jnp.concatenate([_head_weights(qt, h), ident], axis=0)
    qit = qit_ref[0]
    wqi = jnp.concatenate([qit[h * IDX_DIM:(h + 1) * IDX_DIM, :] for h in range(IDX_HEADS)], axis=1)
    wqi = jnp.concatenate([wqi, jnp.zeros_like(wqi)], axis=0)
    wit = wit_ref[0]
    m_ref[...] = jnp.full(m_ref.shape, NEG_INIT, jnp.float32)
    acc_ref[...] = jnp.zeros(acc_ref.shape, jnp.float32)

    half = IDX_HEADS // 2

    def index_logits(j, slot):
        cols = slice(slot * half * BLOCK, (slot + 1) * half * BLOCK)
        lg_ref[slot] = jnp.dot(ki_ref[0, chunk_rows(j), :], wqi[:, cols],
                               preferred_element_type=jnp.float32)

    def weighted_relu(slot):
        lg = lg_ref[slot]
        out = None
        for h in range(half):
            term = jnp.maximum(lg[:, h * BLOCK:(h + 1) * BLOCK], 0.0) * wit[slot * half + h:slot * half + h + 1, :]
            out = term if out is None else out + term
        return out

    group = 32 * SUBLANES

    def index_first_half(j):
        index_logits(j, 1)
        score_ref[...] = weighted_relu(0)

    def index_keys(j):
        index_logits(jnp.minimum(j + 1, idx_chunks - 1), 0)
        score = score_ref[...] + weighted_relu(1)
        bits = pltpu.bitcast(score, jnp.int32)
        skey = bits ^ ((bits >> 31) & 0x7FFFFFFF)
        skey = jnp.where(bits == INT_MIN, 0, skey)
        skey_ref[chunk_rows(j), :] = jnp.where(key_index(j) <= t_row, skey, INT_MIN)

    def index_planes(j, g):
        base = pl.multiple_of(j * kc + g * group, group)
        words = [skey_ref[pl.ds(base + SUBLANES * i, SUBLANES), :] ^ INT_MIN for i in range(32)]
        row0 = pl.multiple_of((j * (kc // group) + g) * SUBLANES, SUBLANES)
        for p, plane in enumerate(_bit_planes(words)):
            planes_ref[p, pl.ds(row0, SUBLANES), :] = plane

    index_stages = [index_first_half, index_keys] + [
        functools.partial(index_planes, g=g) for g in range(kc // group)]

    def scores(j, h, slot):
        pair = slice(LANES * (h // 2), LANES * (h // 2 + 1))
        lhs = jnp.concatenate([k_ref[0, chunk_rows(j), pair], bias_ref[chunk_rows(j), :]], axis=1)
        s = jnp.dot(lhs, wq_ref[h], preferred_element_type=jnp.float32)
        s_ref[slot] = s
        cmax_ref[slot] = _col_reduce(s, jnp.max)

    def accumulate(j, h, slot):
        m_old = m_ref[h]
        m_new = jnp.maximum(m_old, cmax_ref[slot])
        for r0 in range(0, kc, EXP_ROWS):
            p_ref[slot, r0:r0 + EXP_ROWS, :] = jnp.exp2(
                s_ref[slot, r0:r0 + EXP_ROWS, :] - m_new).astype(MXU_DTYPE)
        vt = vt_ref[0, j, V_ROWS * h:V_ROWS * (h + 1), :]
        acc_ref[h] = jnp.exp2(m_old - m_new) * acc_ref[h] + jnp.dot(
            vt, p_ref[slot], preferred_element_type=jnp.float32)
        m_ref[h] = m_new

    index_logits(0, 0)

    @pl.when(t > 0)
    def _():
        scores(0, 0, 0)

    def both(j):
        for h in range(N_KV_HEADS):
            if h + 1 < N_KV_HEADS:
                scores(j, h + 1, (h + 1) % 2)
            else:
                scores(jnp.minimum(j + 1, att_chunks - 1), 0, 0)
            index_stages[h](j)
            accumulate(j, h, h % 2)

    def chunk_step(j, carry):
        both(j)
        return carry

    lax.fori_loop(0, att_chunks, chunk_step, 0)

    @pl.when(idx_chunks > att_chunks)
    def _():
        for stage in index_stages:
            stage(att_chunks)

    @pl.when(t > 0)
    def _():
        outs = []
        for h in range(N_KV_HEADS):
            acc = acc_ref[h]
            o = acc[:HEAD_DIM] / acc[HEAD_DIM:HEAD_DIM + 1]
            outs += [o[:, g * BLOCK:(g + 1) * BLOCK] for g in range(GROUP)]
        o_ref[0] = jnp.concatenate(outs, axis=0).T.astype(MXU_DTYPE)

    zero_row = jnp.zeros((1, BLOCK), jnp.int32)

    def radix_select(rows):
        plane_rows = lax.broadcasted_iota(jnp.int32, (rows, BLOCK), 0)
        active = jnp.where(plane_rows < idx_chunks * (kc // 32), -1, 0)
        cnt_above, thr_u = zero_row, zero_row
        for p in range(32):
            ones = active & planes_ref[p, :rows, :]
            c1 = _popcount_cols(ones)
            take = (cnt_above + c1) >= n_sel
            active = jnp.where(take, ones, active ^ ones)
            cnt_above = jnp.where(take, cnt_above, cnt_above + c1)
            bit = (1 << (31 - p)) - (1 << 32 if p == 0 else 0)
            thr_u = jnp.where(take, thr_u | bit, thr_u)
        sel_ref[0] = thr_u
        sel_ref[1] = cnt_above + _popcount_cols(active)

    class_rows = n_keys // 32 // SELECT_CLASSES
    size_class = (idx_chunks * (kc // 32) - 1) // class_rows
    for c in range(SELECT_CLASSES):
        pl.when(size_class == c)(functools.partial(radix_select, (c + 1) * class_rows))
    thr_u, cnt_ge = sel_ref[0], sel_ref[1]
    thr = jnp.maximum(thr_u ^ INT_MIN, INT_MIN + 1)
    excess = (cnt_ge > n_sel) & (thr_u != 0)
    ties_straddle = jnp.max(jnp.where(excess, 1, 0)) > 0

    def select_from_threshold(j, carry):
        sel = skey_ref[chunk_rows(j), :] >= thr
        bias_ref[chunk_rows(j), :] = jnp.where(sel, 0.0, MASK_NEG).astype(MXU_DTYPE)
        return carry

    lax.fori_loop(0, idx_chunks, select_from_threshold, 0)

    @pl.when(ties_straddle)
    def _():
        def count(pred):
            def body(j, acc):
                hit = jnp.where(pred(skey_ref[chunk_rows(j), :], j), 1, 0)
                return acc + jnp.sum(hit.reshape(kc // SUBLANES, SUBLANES, BLOCK), axis=0)
            acc = lax.fori_loop(0, idx_chunks, body, jnp.zeros((SUBLANES, BLOCK), jnp.int32))
            return jnp.sum(acc, axis=0, keepdims=True)

        need = n_sel - count(lambda sk, j: sk > thr)

        def idx_step(b, x):
            cand = x | lax.shift_left(jnp.int32(1), idx_bits - 1 - b)
            below = count(lambda sk, j: (sk == thr) & (key_index(j) < cand))
            return jnp.where(below < need, cand, x)

        tie_x = jnp.where(excess, lax.fori_loop(0, idx_bits, idx_step, zero_row), n_keys)

        def select_lowest_ties(j, carry):
            sk = skey_ref[chunk_rows(j), :]
            sel = (sk > thr) | ((sk == thr) & (key_index(j) <= tie_x))
            bias_ref[chunk_rows(j), :] = jnp.where(sel, 0.0, MASK_NEG).astype(MXU_DTYPE)
            return carry

        lax.fori_loop(0, idx_chunks, select_lowest_ties, 0)


def _dsa_attention(qt, qit, wit, k, ki, vt):
    B, _, S = qt.shape
    nb = S // BLOCK
    n_sel = min(TOPK_MAX, S // 4)

    def attended(t):
        return jnp.maximum(t - 1, 0)

    def indexed(t):
        return jnp.minimum(t, nb - 1)

    return pl.pallas_call(
        functools.partial(_dsa_kernel, n_sel=n_sel, n_blocks=nb),
        grid=(B, nb + 1),
        in_specs=[pl.BlockSpec((1, QD, BLOCK), lambda b, t: (b, 0, attended(t))),
                  pl.BlockSpec((1, IDX_HEADS * IDX_DIM, BLOCK), lambda b, t: (b, 0, indexed(t))),
                  pl.BlockSpec((1, IDX_HEADS, BLOCK), lambda b, t: (b, 0, indexed(t))),
                  pl.BlockSpec((1, S, KVD), lambda b, t: (b, 0, 0)),
                  pl.BlockSpec((1, S, 2 * IDX_DIM), lambda b, t: (b, 0, 0)),
                  pl.BlockSpec((1, S // TOK_TILE, N_KV_HEADS * V_ROWS, TOK_TILE), lambda b, t: (b, 0, 0, 0))],
        out_specs=pl.BlockSpec((1, BLOCK, QD), lambda b, t: (b, attended(t), 0)),
        out_shape=jax.ShapeDtypeStruct((B, S, QD), MXU_DTYPE),
        scratch_shapes=[pltpu.VMEM((S, BLOCK), jnp.int32),
                        pltpu.VMEM((32, S // 32, BLOCK), jnp.int32),
                        pltpu.VMEM((N_KV_HEADS, 2 * LANES, GROUP * BLOCK), MXU_DTYPE),
                        pltpu.VMEM((N_KV_HEADS, 1, GROUP * BLOCK), jnp.float32),
                        pltpu.VMEM((N_KV_HEADS, V_ROWS, GROUP * BLOCK), jnp.float32),
                        pltpu.VMEM((2, 1, BLOCK), jnp.int32),
                        pltpu.VMEM((2, TOK_TILE, GROUP * BLOCK), jnp.float32),
                        pltpu.VMEM((2, TOK_TILE, GROUP * BLOCK), MXU_DTYPE),
                        pltpu.VMEM((2, 1, GROUP * BLOCK), jnp.float32),
                        pltpu.VMEM((S, BLOCK), MXU_DTYPE),
                        pltpu.VMEM((2, TOK_TILE, IDX_HEADS // 2 * BLOCK), jnp.float32),
                        pltpu.VMEM((TOK_TILE, BLOCK), jnp.float32)],
        compiler_params=pltpu.CompilerParams(
            dimension_semantics=("arbitrary", "arbitrary"), vmem_limit_bytes=VMEM_LIMIT),
        name="dsa_attention",
    )(qt, qit, wit, k, ki, vt)


def _post_kernel(a_ref, x_ref, g_ref, wo_ref, wu_ref, wd_ref, o_ref):
    mix = jnp.dot(a_ref[...], wo_ref[...], preferred_element_type=jnp.float32)
    x1 = x_ref[...] + _rms(mix, g_ref[1:2, :])
    hn = _rms(x1, g_ref[2:3, :]).astype(MXU_DTYPE)
    acc = jnp.zeros(x1.shape, jnp.float32)
    for c in range(D_FF // FF_CHUNK):
        cols = slice(c * FF_CHUNK, (c + 1) * FF_CHUNK)
        a = jnp.maximum(jnp.dot(hn, wu_ref[:, cols], preferred_element_type=jnp.float32), 0.0)
        acc = acc + jnp.dot((a * a).astype(MXU_DTYPE), wd_ref[cols, :],
                            preferred_element_type=jnp.float32)
    o_ref[...] = x1 + _rms(acc, g_ref[3:4, :])


def _post_attention(attn, x, g, w_out, w_up, w_down):
    T, D = x.shape
    tm = ROW_TILE
    return pl.pallas_call(
        _post_kernel,
        grid=(T // tm,),
        in_specs=[pl.BlockSpec((tm, QD), lambda i: (i, 0)),
                  pl.BlockSpec((tm, D), lambda i: (i, 0)),
                  pl.BlockSpec((4, D), lambda i: (0, 0)),
                  pl.BlockSpec((QD, D), lambda i: (0, 0), pipeline_mode=pl.Buffered(1)),
                  pl.BlockSpec((D, D_FF), lambda i: (0, 0), pipeline_mode=pl.Buffered(1)),
                  pl.BlockSpec((D_FF, D), lambda i: (0, 0), pipeline_mode=pl.Buffered(1))],
        out_specs=pl.BlockSpec((tm, D), lambda i: (i, 0)),
        out_shape=jax.ShapeDtypeStruct((T, D), jnp.float32),
        compiler_params=pltpu.CompilerParams(
            dimension_semantics=("arbitrary",), vmem_limit_bytes=VMEM_LIMIT),
        name="post_attention",
    )(attn, x, g, w_out, w_up, w_down)


def _pad_rows(w, rows):
    return jnp.concatenate([w, jnp.zeros((rows - w.shape[0], w.shape[1]), w.dtype)], axis=0)


def kernel(x, positions, norm_gains, w_mlp_up, w_mlp_down, a_w_in, a_w_out, a_sinks, b_w_in, b_w_out):
    B, S, D = x.shape
    bf = MXU_DTYPE
    cos_t, sin_t = _rope_tables(positions)

    wt_a = a_w_in[0].T.astype(bf)
    qt, k, vt = _project(x, norm_gains[0, 0], wt_a, cos_t, sin_t, with_indexer=False)
    attn = _swa_attention(qt, k, vt, a_sinks[0])
    x = _post_attention(attn.reshape(B * S, QD), x.reshape(B * S, D), norm_gains[0],
                        a_w_out[0].astype(bf), w_mlp_up[0].astype(bf), w_mlp_down[0].astype(bf))
    x = x.reshape(B, S, D)

    wb = b_w_in[0].T
    r_qi = QD + 2 * KVD
    r_ki = r_qi + IDX_HEADS * IDX_DIM
    r_wi = r_ki + IDX_DIM
    wt_b = jnp.concatenate([wb[:r_ki], _pad_rows(wb[r_ki:r_wi], 2 * IDX_DIM),
                            _pad_rows(wb[r_wi:], 2 * SUBLANES)], axis=0).astype(bf)
    qt, k, vt, qit, ki, wit = _project(x, norm_gains[1, 0], wt_b, cos_t, sin_t, with_indexer=True)
    attn = _dsa_attention(qt, qit, wit, k, ki, vt)
    x = _post_attention(attn.reshape(B * S, QD), x.reshape(B * S, D), norm_gains[1],
                        b_w_out[0].astype(bf), w_mlp_up[1].astype(bf), w_mlp_down[1].astype(bf))
    return x.reshape(B, S, D)
```

```python
import functools
import math

import jax
import jax.numpy as jnp
from jax import lax
from jax.experimental import pallas as pl
from jax.experimental.pallas import tpu as pltpu

D_MODEL = 1024
HEAD_DIM = 64
N_Q_HEADS = 16
N_KV_HEADS = 4
GROUP = 4
ROT_DIM = 16
ROT_HALF = ROT_DIM // 2
ROPE_THETA = 500000.0
BLOCK = 128
IDX_HEADS = 8
IDX_DIM = 64
TOPK_MAX = 256
D_FF = 4 * D_MODEL
EPS = 1e-6
QD = N_Q_HEADS * HEAD_DIM
KVD = N_KV_HEADS * HEAD_DIM

LANES = 128
SUBLANES = 8
PACKED_ROWS = 16
TOK_TILE = 512
ROW_TILE = 512
FF_CHUNK = 1024
EXP_ROWS = 64
SWA_BLOCKS = 4
SELECT_CLASSES = 4
VMEM_LIMIT = 56 * 1024 * 1024
V_ROWS = HEAD_DIM + PACKED_ROWS
INT_MIN = -2 ** 31
NEG_INIT = -1e30
MASK_NEG = -1e9
LOG2E = math.log2(math.e)
MXU_DTYPE = jnp.bfloat16

_NT = (((1,), (1,)), ((), ()))


def _rms(x, g):
    return x * lax.rsqrt(jnp.mean(x * x, axis=-1, keepdims=True) + EPS) * g


def _rope_table_kernel(pos_ref, inv_ref, cos_ref, sin_ref):
    ang = pos_ref[0].astype(jnp.float32) * inv_ref[:, :1]
    cos_ref[0] = jnp.cos(ang)
    sin_ref[0] = jnp.sin(ang)


def _rope_tables(positions):
    B, S = positions.shape
    inv = ROPE_THETA ** (-jnp.arange(0, ROT_DIM, 2, dtype=jnp.float32) / ROT_DIM)
    inv = jnp.broadcast_to(inv[:, None], (ROT_HALF, LANES))
    out = jax.ShapeDtypeStruct((B, ROT_HALF, S), jnp.float32)
    return pl.pallas_call(
        _rope_table_kernel,
        grid=(B,),
        in_specs=[pl.BlockSpec((1, 1, S), lambda b: (b, 0, 0)),
                  pl.BlockSpec((ROT_HALF, LANES), lambda b: (0, 0))],
        out_specs=[pl.BlockSpec((1, ROT_HALF, S), lambda b: (b, 0, 0))] * 2,
        out_shape=[out, out],
        name="rope_tables",
    )(positions.reshape(B, 1, S), inv)


def _rope_heads_t(y, n_heads, cos, sin, scale=None):
    parts = []
    for h in range(n_heads):
        r0 = h * HEAD_DIM
        x1 = y[r0:r0 + ROT_HALF]
        x2 = y[r0 + ROT_HALF:r0 + ROT_DIM]
        parts += [x1 * cos - x2 * sin, x2 * cos + x1 * sin, y[r0 + ROT_DIM:r0 + HEAD_DIM]]
    out = jnp.concatenate(parts, axis=0)
    return out if scale is None else out * scale


def _proj_kernel(x_ref, g_ref, wt_ref, cos_ref, sin_ref, *out_refs, with_indexer):
    hn = _rms(x_ref[0], g_ref[...]).astype(MXU_DTYPE)
    tm = hn.shape[0]
    cos = cos_ref[0]
    sin = sin_ref[0]

    def proj(r0, rows):
        return lax.dot_general(wt_ref[r0:r0 + rows, :], hn, _NT,
                               preferred_element_type=jnp.float32)

    qt_ref, k_ref, vt_ref = out_refs[:3]
    q = proj(0, QD)
    qt_ref[0] = _rope_heads_t(q, N_Q_HEADS, cos, sin, HEAD_DIM ** -0.5 * LOG2E).astype(MXU_DTYPE)
    k = _rope_heads_t(proj(QD, KVD), N_KV_HEADS, cos, sin)
    k_ref[0] = k.T.astype(MXU_DTYPE)
    v = proj(QD + KVD, KVD)
    ones = jnp.ones((PACKED_ROWS, tm), jnp.float32)
    v_rows = []
    for h in range(N_KV_HEADS):
        v_rows += [v[h * HEAD_DIM:(h + 1) * HEAD_DIM], ones]
    vt_ref[0, 0] = jnp.concatenate(v_rows, axis=0).astype(MXU_DTYPE)
    if with_indexer:
        qit_ref, ki_ref, wit_ref = out_refs[3:]
        r0 = QD + 2 * KVD
        qi = _rope_heads_t(proj(r0, IDX_HEADS * IDX_DIM), IDX_HEADS, cos, sin)
        qit_ref[0] = qi.astype(MXU_DTYPE)
        r0 += IDX_HEADS * IDX_DIM
        ki = proj(r0, 2 * IDX_DIM)
        ki = jnp.concatenate([_rope_heads_t(ki[:IDX_DIM], 1, cos, sin), ki[IDX_DIM:]], axis=0)
        ki_ref[0] = ki.T.astype(MXU_DTYPE)
        r0 += 2 * IDX_DIM
        wi = proj(r0, 2 * SUBLANES)
        wit_ref[0] = wi[:IDX_HEADS] * ((IDX_HEADS ** -0.5) * (IDX_DIM ** -0.5))


def _project(x, g, wt, cos_t, sin_t, with_indexer):
    B, S, D = x.shape
    tm = TOK_TILE
    n_rows = wt.shape[0]
    bf = MXU_DTYPE
    vr = N_KV_HEADS * V_ROWS
    out_shape = [jax.ShapeDtypeStruct((B, QD, S), bf),
                 jax.ShapeDtypeStruct((B, S, KVD), bf),
                 jax.ShapeDtypeStruct((B, S // tm, vr, tm), bf)]
    out_specs = [pl.BlockSpec((1, QD, tm), lambda b, i: (b, 0, i)),
                 pl.BlockSpec((1, tm, KVD), lambda b, i: (b, i, 0)),
                 pl.BlockSpec((1, 1, vr, tm), lambda b, i: (b, i, 0, 0))]
    if with_indexer:
        out_shape += [jax.ShapeDtypeStruct((B, IDX_HEADS * IDX_DIM, S), bf),
                      jax.ShapeDtypeStruct((B, S, 2 * IDX_DIM), bf),
                      jax.ShapeDtypeStruct((B, IDX_HEADS, S), jnp.float32)]
        out_specs += [pl.BlockSpec((1, IDX_HEADS * IDX_DIM, tm), lambda b, i: (b, 0, i)),
                      pl.BlockSpec((1, tm, 2 * IDX_DIM), lambda b, i: (b, i, 0)),
                      pl.BlockSpec((1, IDX_HEADS, tm), lambda b, i: (b, 0, i))]
    return pl.pallas_call(
        functools.partial(_proj_kernel, with_indexer=with_indexer),
        grid=(B, S // tm),
        in_specs=[pl.BlockSpec((1, tm, D), lambda b, i: (b, i, 0)),
                  pl.BlockSpec((1, D), lambda b, i: (0, 0)),
                  pl.BlockSpec((n_rows, D), lambda b, i: (0, 0)),
                  pl.BlockSpec((1, ROT_HALF, tm), lambda b, i: (b, 0, i)),
                  pl.BlockSpec((1, ROT_HALF, tm), lambda b, i: (b, 0, i))],
        out_specs=out_specs,
        out_shape=out_shape,
        compiler_params=pltpu.CompilerParams(
            dimension_semantics=("arbitrary", "arbitrary"), vmem_limit_bytes=VMEM_LIMIT),
        name="proj_indexer" if with_indexer else "proj",
    )(x, g.reshape(1, D), wt, cos_t, sin_t)


def _head_weights(qt, h):
    blocks = [qt[(GROUP * h + g) * HEAD_DIM:(GROUP * h + g + 1) * HEAD_DIM, :] for g in range(GROUP)]
    w = jnp.concatenate(blocks, axis=1)
    z = jnp.zeros_like(w)
    return jnp.concatenate([w, z] if h % 2 == 0 else [z, w], axis=0)


def _bias_identity():
    r = lax.broadcasted_iota(jnp.int32, (LANES, GROUP * BLOCK), 0)
    c = lax.broadcasted_iota(jnp.int32, (LANES, GROUP * BLOCK), 1)
    return jnp.where(r == (c & (BLOCK - 1)), 1.0, 0.0).astype(MXU_DTYPE)


def _col_reduce(x, op):
    rows, n = x.shape
    part = op(x.reshape(rows // SUBLANES, SUBLANES, n), axis=0)
    return op(part, axis=0, keepdims=True)


def _swa_kernel(qt_ref, kp_ref, kc_ref, vp_ref, vc_ref, sink_ref, o_ref, s_ref):
    n = pl.program_id(1)
    c = lax.broadcasted_iota(jnp.int32, (2 * BLOCK, BLOCK), 0)
    i = lax.broadcasted_iota(jnp.int32, (2 * BLOCK, BLOCK), 1)
    band = (c > i) & (c <= i + BLOCK)
    first_slot = jnp.where(n > 0, 0, BLOCK)
    bias_first = jnp.where(band & (c >= first_slot), 0.0, MASK_NEG).astype(MXU_DTYPE)
    bias_rest = jnp.where(band, 0.0, MASK_NEG).astype(MXU_DTYPE)
    ident = _bias_identity()
    units = [(blk, h) for blk in range(SWA_BLOCKS) for h in range(N_KV_HEADS)]

    def window(cur_ref, prev_ref, blk, rows, axis):
        def take(ref, b):
            span = slice(b * BLOCK, (b + 1) * BLOCK)
            return ref[0, span, rows] if axis == 0 else ref[0, 0, rows, span]
        prev = take(cur_ref, blk - 1) if blk > 0 else take(prev_ref, 0)
        return jnp.concatenate([prev, take(cur_ref, blk)], axis=axis)

    def scores(u):
        blk, h = units[u]
        pair = slice(LANES * (h // 2), LANES * (h // 2 + 1))
        kcat = window(kc_ref, kp_ref, blk, pair, 0)
        qt = qt_ref[0, :, blk * BLOCK:(blk + 1) * BLOCK]
        w = jnp.concatenate([_head_weights(qt, h), ident], axis=0)
        bias = bias_first if blk == 0 else bias_rest
        s_ref[u % 2] = jnp.dot(jnp.concatenate([kcat, bias], axis=1), w,
                               preferred_element_type=jnp.float32)

    def finish(u):
        blk, h = units[u]
        s = s_ref[u % 2]
        sink = sink_ref[:, GROUP * h * BLOCK:GROUP * (h + 1) * BLOCK]
        m = jnp.maximum(_col_reduce(s, jnp.max), sink)
        p = jnp.exp2(s - m).astype(MXU_DTYPE)
        vcat = window(vc_ref, vp_ref, blk, slice(V_ROWS * h, V_ROWS * (h + 1)), 1)
        o = jnp.dot(vcat, p, preferred_element_type=jnp.float32)
        o = o[:HEAD_DIM] / (o[HEAD_DIM:HEAD_DIM + 1] + jnp.exp2(sink - m))
        return [o[:, g * BLOCK:(g + 1) * BLOCK] for g in range(GROUP)]

    outs = []
    scores(0)
    for u, (blk, h) in enumerate(units):
        if u + 1 < len(units):
            scores(u + 1)
        outs += finish(u)
        if h == N_KV_HEADS - 1:
            o_ref[0, blk * BLOCK:(blk + 1) * BLOCK, :] = jnp.concatenate(outs, axis=0).T.astype(MXU_DTYPE)
            outs = []


def _swa_attention(qt, k, vt, sinks):
    B, _, S = qt.shape
    qb = SWA_BLOCKS * BLOCK
    per_tile = TOK_TILE // qb
    blocks_per_tile = TOK_TILE // BLOCK
    vr = N_KV_HEADS * V_ROWS
    sink_row = jnp.repeat(sinks.astype(jnp.float32) * LOG2E, BLOCK).reshape(1, N_Q_HEADS * BLOCK)

    def prev(n):
        return jnp.maximum(n * SWA_BLOCKS - 1, 0)

    return pl.pallas_call(
        _swa_kernel,
        grid=(B, S // qb),
        in_specs=[pl.BlockSpec((1, QD, qb), lambda b, n: (b, 0, n)),
                  pl.BlockSpec((1, BLOCK, KVD), lambda b, n: (b, prev(n), 0)),
                  pl.BlockSpec((1, qb, KVD), lambda b, n: (b, n, 0)),
                  pl.BlockSpec((1, 1, vr, BLOCK),
                               lambda b, n: (b, prev(n) // blocks_per_tile, 0, prev(n) % blocks_per_tile)),
                  pl.BlockSpec((1, 1, vr, qb), lambda b, n: (b, n // per_tile, 0, n % per_tile)),
                  pl.BlockSpec((1, N_Q_HEADS * BLOCK), lambda b, n: (0, 0))],
        out_specs=pl.BlockSpec((1, qb, QD), lambda b, n: (b, n, 0)),
        out_shape=jax.ShapeDtypeStruct((B, S, QD), MXU_DTYPE),
        scratch_shapes=[pltpu.VMEM((2, 2 * BLOCK, GROUP * BLOCK), jnp.float32)],
        compiler_params=pltpu.CompilerParams(
            dimension_semantics=("arbitrary", "arbitrary"), vmem_limit_bytes=VMEM_LIMIT),
        name="swa_attention",
    )(qt, k, k, vt, vt, sink_row)


def _bit_planes(words):
    a = list(words)
    j, m = 16, 0x0000FFFF
    while j:
        k = 0
        while k < 32:
            t = (a[k] ^ lax.shift_right_logical(a[k + j], jnp.int32(j))) & m
            a[k] = a[k] ^ t
            a[k + j] = a[k + j] ^ (t << j)
            k = (k + j + 1) & ~j
        j >>= 1
        m ^= (m << j) & 0xFFFFFFFF
    return a


def _popcount_cols(x):
    return _col_reduce(lax.population_count(x), jnp.sum)


def _dsa_kernel(qt_ref, qit_ref, wit_ref, k_ref, ki_ref, vt_ref, o_ref,
                skey_ref, planes_ref, wq_ref, m_ref, acc_ref, sel_ref,
                s_ref, p_ref, cmax_ref, bias_ref, lg_ref, score_ref, *, n_sel, n_blocks):
    t = pl.program_id(1)
    kc = TOK_TILE
    per_chunk = kc // BLOCK
    n_idx = jnp.minimum(t, n_blocks - 1)
    idx_chunks = n_idx // per_chunk + 1
    att_chunks = (t + per_chunk - 1) // per_chunk
    n_keys = skey_ref.shape[0]
    idx_bits = (n_keys - 1).bit_length()
    t_row = n_idx * BLOCK + lax.broadcasted_iota(jnp.int32, (1, BLOCK), 1)

    def chunk_rows(j):
        return pl.ds(pl.multiple_of(j * kc, kc), kc)

    def key_index(j):
        return j * kc + lax.broadcasted_iota(jnp.int32, (kc, BLOCK), 0)

    @pl.when((pl.program_id(0) == 0) & (t == 0))
    def _():
        planes_ref[...] = jnp.zeros(planes_ref.shape, jnp.int32)
        bias_ref[...] = jnp.zeros(bias_ref.shape, MXU_DTYPE)

    qt = qt_ref[0]
    ident = _bias_identity()
    for h in range(N_KV_HEADS):
        wq_ref[h] = jnp.concatenate([_head_weights(qt, h), ident], axis=0)
    qit = qit_ref[0]
    wqi = jnp.concatenate([qit[h * IDX_DIM:(h + 1) * IDX_DIM, :] for h in range(IDX_HEADS)], axis=1)
    wqi = jnp.concatenate([wqi, jnp.zeros_like(wqi)], axis=0)
    wit = wit_ref[0]
    m_ref[...] = jnp.full(m_ref.shape, NEG_INIT, jnp.float32)
    acc_ref[...] = jnp.zeros(acc_ref.shape, jnp.float32)

    half = IDX_HEADS // 2

    def index_logits(j, slot):
        cols = slice(slot * half * BLOCK, (slot + 1) * half * BLOCK)
        lg_ref[slot] = jnp.dot(ki_ref[0, chunk_rows(j), :], wqi[:, cols],
                               preferred_element_type=jnp.float32)

    def weighted_relu(slot):
        lg = lg_ref[slot]
        out = None
        for h in range(half):
            term = jnp.maximum(lg[:, h * BLOCK:(h + 1) * BLOCK], 0.0) * wit[slot * half + h:slot * half + h + 1, :]
            out = term if out is None else out + term
        return out

    group = 32 * SUBLANES

    def index_first_half(j):
        index_logits(j, 1)
        score_ref[...] = weighted_relu(0)

    def index_keys(j):
        index_logits(jnp.minimum(j + 1, idx_chunks - 1), 0)
        score = score_ref[...] + weighted_relu(1)
        bits = pltpu.bitcast(score, jnp.int32)
        skey = bits ^ ((bits >> 31) & 0x7FFFFFFF)
        skey = jnp.where(bits == INT_MIN, 0, skey)
        skey_ref[chunk_rows(j), :] = jnp.where(key_index(j) <= t_row, skey, INT_MIN)

    def index_planes(j, g):
        base = pl.multiple_of(j * kc + g * group, group)
        words = [skey_ref[pl.ds(base + SUBLANES * i, SUBLANES), :] ^ INT_MIN for i in range(32)]
        row0 = pl.multiple_of((j * (kc // group) + g) * SUBLANES, SUBLANES)
        for p, plane in enumerate(_bit_planes(words)):
            planes_ref[p, pl.ds(row0, SUBLANES), :] = plane

    index_stages = [index_first_half, index_keys] + [
        functools.partial(index_planes, g=g) for g in range(kc // group)]

    def scores(j, h, slot):
        pair = slice(LANES * (h // 2), LANES * (h // 2 + 1))
        lhs = jnp.concatenate([k_ref[0, chunk_rows(j), pair], bias_ref[chunk_rows(j), :]], axis=1)
        s = jnp.dot(lhs, wq_ref[h], preferred_element_type=jnp.float32)
        s_ref[slot] = s
        cmax_ref[slot] = _col_reduce(s, jnp.max)

    def accumulate(j, h, slot):
        m_old = m_ref[h]
        m_new = jnp.maximum(m_old, cmax_ref[slot])
        for r0 in range(0, kc, EXP_ROWS):
            p_ref[slot, r0:r0 + EXP_ROWS, :] = jnp.exp2(
                s_ref[slot, r0:r0 + EXP_ROWS, :] - m_new).astype(MXU_DTYPE)
        vt = vt_ref[0, j, V_ROWS * h:V_ROWS * (h + 1), :]
        acc_ref[h] = jnp.exp2(m_old - m_new) * acc_ref[h] + jnp.dot(
            vt, p_ref[slot], preferred_element_type=jnp.float32)
        m_ref[h] = m_new

    index_logits(0, 0)
    scores(0, 0, 0)

    def both(j):
        for h in range(N_KV_HEADS):
            if h + 1 < N_KV_HEADS:
                scores(j, h + 1, (h + 1) % 2)
            else:
                scores(jnp.minimum(j + 1, att_chunks - 1), 0, 0)
            index_stages[h](j)
            accumulate(j, h, h % 2)

    def chunk_step(j, carry):
        both(j)
        return carry

    lax.fori_loop(0, att_chunks, chunk_step, 0)

    @pl.when(idx_chunks > att_chunks)
    def _():
        for stage in index_stages:
            stage(att_chunks)

    @pl.when(t > 0)
    def _():
        outs = []
        for h in range(N_KV_HEADS):
            acc = acc_ref[h]
            o = acc[:HEAD_DIM] / acc[HEAD_DIM:HEAD_DIM + 1]
            outs += [o[:, g * BLOCK:(g + 1) * BLOCK] for g in range(GROUP)]
        o_ref[0] = jnp.concatenate(outs, axis=0).T.astype(MXU_DTYPE)

    zero_row = jnp.zeros((1, BLOCK), jnp.int32)

    def radix_select(rows):
        plane_rows = lax.broadcasted_iota(jnp.int32, (rows, BLOCK), 0)
        active = jnp.where(plane_rows < idx_chunks * (kc // 32), -1, 0)
        cnt_above, thr_u = zero_row, zero_row
        for p in range(32):
            ones = active & planes_ref[p, :rows, :]
            c1 = _popcount_cols(ones)
            take = (cnt_above + c1) >= n_sel
            active = jnp.where(take, ones, active ^ ones)
            cnt_above = jnp.where(take, cnt_above, cnt_above + c1)
            bit = (1 << (31 - p)) - (1 << 32 if p == 0 else 0)
            thr_u = jnp.where(take, thr_u | bit, thr_u)
        sel_ref[0] = thr_u
        sel_ref[1] = cnt_above + _popcount_cols(active)

    class_rows = n_keys // 32 // SELECT_CLASSES
    size_class = (idx_chunks * (kc // 32) - 1) // class_rows
    for c in range(SELECT_CLASSES):
        pl.when(size_class == c)(functools.partial(radix_select, (c + 1) * class_rows))
    thr_u, cnt_ge = sel_ref[0], sel_ref[1]
    thr = jnp.maximum(thr_u ^ INT_MIN, INT_MIN + 1)
    excess = (cnt_ge > n_sel) & (thr_u != 0)
    ties_straddle = jnp.max(jnp.where(excess, 1, 0)) > 0

    def select_from_threshold(j, carry):
        sel = skey_ref[chunk_rows(j), :] >= thr
        bias_ref[chunk_rows(j), :] = jnp.where(sel, 0.0, MASK_NEG).astype(MXU_DTYPE)
        return carry

    lax.fori_loop(0, idx_chunks, select_from_threshold, 0)

    @pl.when(ties_straddle)
    def _():
        def count(pred):
            def body(j, acc):
                hit = jnp.where(pred(skey_ref[chunk_rows(j), :], j), 1, 0)
                return acc + jnp.sum(hit.reshape(kc // SUBLANES, SUBLANES, BLOCK), axis=0)
            acc = lax.fori_loop(0, idx_chunks, body, jnp.zeros((SUBLANES, BLOCK), jnp.int32))
            return jnp.sum(acc, axis=0, keepdims=True)

        need = n_sel - count(lambda sk, j: sk > thr)

        def idx_step(b, x):
            cand = x | lax.shift_left(jnp.int32(1), idx_bits - 1 - b)
            below = count(lambda sk, j: (sk == thr) & (key_index(j) < cand))
            return jnp.where(below < need, cand, x)

        tie_x = jnp.where(excess, lax.fori_loop(0, idx_bits, idx_step, zero_row), n_keys)

        def select_lowest_ties(j, carry):
            sk = skey_ref[chunk_rows(j), :]
            sel = (sk > thr) | ((sk == thr) & (key_index(j) <= tie_x))
            bias_ref[chunk_rows(j), :] = jnp.where(sel, 0.0, MASK_NEG).astype(MXU_DTYPE)
            return carry

        lax.fori_loop(0, idx_chunks, select_lowest_ties, 0)


def _dsa_attention(qt, qit, wit, k, ki, vt):
    B, _, S = qt.shape
    nb = S // BLOCK
    n_sel = min(TOPK_MAX, S // 4)

    def attended(t):
        return jnp.maximum(t - 1, 0)

    def indexed(t):
        return jnp.minimum(t, nb - 1)

    return pl.pallas_call(
        functools.partial(_dsa_kernel, n_sel=n_sel, n_blocks=nb),
        grid=(B, nb + 1),
        in_specs=[pl.BlockSpec((1, QD, BLOCK), lambda b, t: (b, 0, attended(t))),
                  pl.BlockSpec((1, IDX_HEADS * IDX_DIM, BLOCK), lambda b, t: (b, 0, indexed(t))),
                  pl.BlockSpec((1, IDX_HEADS, BLOCK), lambda b, t: (b, 0, indexed(t))),
                  pl.BlockSpec((1, S, KVD), lambda b, t: (b, 0, 0)),
                  pl.BlockSpec((1, S, 2 * IDX_DIM), lambda b, t: (b, 0, 0)),
                  pl.BlockSpec((1, S // TOK_TILE, N_KV_HEADS * V_ROWS, TOK_TILE), lambda b, t: (b, 0, 0, 0))],
        out_specs=pl.BlockSpec((1, BLOCK, QD), lambda b, t: (b, attended(t), 0)),
        out_shape=jax.ShapeDtypeStruct((B, S, QD), MXU_DTYPE),
        scratch_shapes=[pltpu.VMEM((S, BLOCK), jnp.int32),
                        pltpu.VMEM((32, S // 32, BLOCK), jnp.int32),
                        pltpu.VMEM((N_KV_HEADS, 2 * LANES, GROUP * BLOCK), MXU_DTYPE),
                        pltpu.VMEM((N_KV_HEADS, 1, GROUP * BLOCK), jnp.float32),
                        pltpu.VMEM((N_KV_HEADS, V_ROWS, GROUP * BLOCK), jnp.float32),
                        pltpu.VMEM((2, 1, BLOCK), jnp.int32),
                        pltpu.VMEM((2, TOK_TILE, GROUP * BLOCK), jnp.float32),
                        pltpu.VMEM((2, TOK_TILE, GROUP * BLOCK), MXU_DTYPE),
                        pltpu.VMEM((2, 1, GROUP * BLOCK), jnp.float32),
                        pltpu.VMEM((S, BLOCK), MXU_DTYPE),
                        pltpu.VMEM((2, TOK_TILE, IDX_HEADS // 2 * BLOCK), jnp.float32),
                        pltpu.VMEM((TOK_TILE, BLOCK), jnp.float32)],
        compiler_params=pltpu.CompilerParams(
            dimension_semantics=("arbitrary", "arbitrary"), vmem_limit_bytes=VMEM_LIMIT),
        name="dsa_attention",
    )(qt, qit, wit, k, ki, vt)


def _post_kernel(a_ref, x_ref, g_ref, wo_ref, wu_ref, wd_ref, o_ref):
    mix = jnp.dot(a_ref[...], wo_ref[...], preferred_element_type=jnp.float32)
    x1 = x_ref[...] + _rms(mix, g_ref[1:2, :])
    hn = _rms(x1, g_ref[2:3, :]).astype(MXU_DTYPE)
    acc = jnp.zeros(x1.shape, jnp.float32)
    for c in range(D_FF // FF_CHUNK):
        cols = slice(c * FF_CHUNK, (c + 1) * FF_CHUNK)
        a = jnp.maximum(jnp.dot(hn, wu_ref[:, cols], preferred_element_type=jnp.float32), 0.0)
        acc = acc + jnp.dot((a * a).astype(MXU_DTYPE), wd_ref[cols, :],
                            preferred_element_type=jnp.float32)
    o_ref[...] = x1 + _rms(acc, g_ref[3:4, :])


def _post_attention(attn, x, layer, gains, w_out, w_up, w_down):
    T, D = x.shape
    tm = ROW_TILE
    resident = dict(pipeline_mode=pl.Buffered(1))
    return pl.pallas_call(
        _post_kernel,
        grid=(T // tm,),
        in_specs=[pl.BlockSpec((tm, QD), lambda i: (i, 0)),
                  pl.BlockSpec((tm, D), lambda i: (i, 0)),
                  pl.BlockSpec((None, 4, D), lambda i: (layer, 0, 0)),
                  pl.BlockSpec((None, QD, D), lambda i: (0, 0, 0), **resident),
                  pl.BlockSpec((None, D, D_FF), lambda i: (layer, 0, 0), **resident),
                  pl.BlockSpec((None, D_FF, D), lambda i: (layer, 0, 0), **resident)],
        out_specs=pl.BlockSpec((tm, D), lambda i: (i, 0)),
        out_shape=jax.ShapeDtypeStruct((T, D), jnp.float32),
        compiler_params=pltpu.CompilerParams(
            dimension_semantics=("arbitrary",), vmem_limit_bytes=VMEM_LIMIT),
        name="post_attention",
    )(attn, x, gains, w_out, w_up, w_down)


def _pad_rows(w, rows):
    return jnp.concatenate([w, jnp.zeros((rows - w.shape[0], w.shape[1]), w.dtype)], axis=0)


def kernel(x, positions, norm_gains, w_mlp_up, w_mlp_down, a_w_in, a_w_out, a_sinks, b_w_in, b_w_out):
    B, S, D = x.shape
    bf = MXU_DTYPE
    cos_t, sin_t = _rope_tables(positions)
    w_up, w_down = w_mlp_up.astype(bf), w_mlp_down.astype(bf)

    wt_a = a_w_in[0].T.astype(bf)
    qt, k, vt = _project(x, norm_gains[0, 0], wt_a, cos_t, sin_t, with_indexer=False)
    attn = _swa_attention(qt, k, vt, a_sinks[0])
    x = _post_attention(attn.reshape(B * S, QD), x.reshape(B * S, D), 0, norm_gains,
                        a_w_out.astype(bf), w_up, w_down)
    x = x.reshape(B, S, D)

    wb = b_w_in[0].T
    r_qi = QD + 2 * KVD
    r_ki = r_qi + IDX_HEADS * IDX_DIM
    r_wi = r_ki + IDX_DIM
    wt_b = jnp.concatenate([wb[:r_ki], _pad_rows(wb[r_ki:r_wi], 2 * IDX_DIM),
                            _pad_rows(wb[r_wi:], 2 * SUBLANES)], axis=0).astype(bf)
    qt, k, vt, qit, ki, wit = _project(x, norm_gains[1, 0], wt_b, cos_t, sin_t, with_indexer=True)
    attn = _dsa_attention(qt, qit, wit, k, ki, vt)
    x = _post_attention(attn.reshape(B * S, QD), x.reshape(B * S, D), 1, norm_gains,
                        b_w_out.astype(bf), w_up, w_down)
    return x.reshape(B, S, D)
```
